```python
import jax, jax.numpy as jnp
from jax import lax
import numpy as np

D_MODEL = 2048
BATCH = 2
SEQ = 8192
DEPTH = 4

GRID_W = 64
CTX_LEN = 256
HEAD_DIM = 128
N_HEADS_A = 8
N_KV_A = 2
N_HEADS_B = 8
N_KV_B = 2
WINDOW = 128
Q_BLOCK = 128
ROPE_THETA = 10000.0
MLA_HEADS = 16
MLA_Q_RANK = 512
MLA_KV_RANK = 256
MLA_NOPE = 128
MLA_ROPE = 64
MLA_V = 128
D_FF = 5632
N_MOD = 9
EPS = 1e-6
NEG_INF = -1e30
N_EVEN = (DEPTH + 1) // 2
N_ODD = DEPTH // 2

QA = N_HEADS_A * HEAD_DIM
KA = N_KV_A * HEAD_DIM
QB = N_HEADS_B * HEAD_DIM
KB = N_KV_B * HEAD_DIM
AB_IN = QA + 2 * KA + QB + 2 * KB
AB_SPLITS = [QA, QA + KA, QA + 2 * KA, QA + 2 * KA + QB, QA + 2 * KA + QB + KB]
AB_OUT = (N_HEADS_A + N_HEADS_B) * HEAD_DIM
C_IN = MLA_Q_RANK + MLA_KV_RANK + MLA_ROPE
MLA_Q_OUT = MLA_HEADS * (MLA_NOPE + MLA_ROPE)
MLA_KV_OUT = MLA_HEADS * (MLA_NOPE + MLA_V)
MLA_OUT = MLA_HEADS * MLA_V

kernel_name = "hybrid_flow_backbone_gqa_swa_mla_macaron"


def rmsnorm(x, g):
    xf = x.astype(jnp.float32)
    y = xf * lax.rsqrt(jnp.mean(xf * xf, axis=-1, keepdims=True) + EPS)
    return (y * g.astype(jnp.float32)).astype(x.dtype)


def adaln_in(h, g, mod, j):
    return rmsnorm(h, g) * (1.0 + mod[..., 3 * j + 1, :]) + mod[..., 3 * j, :]


def swiglu(h, w_gu, w_dn):
    g, u = jnp.split(h @ w_gu, 2, axis=-1)
    return (jax.nn.silu(g) * u) @ w_dn


def ffn_sub(h, mod, j, g, w_gu, w_dn):
    return h + 0.5 * mod[..., 3 * j + 2, :] * swiglu(adaln_in(h, g, mod, j), w_gu, w_dn)


def grid_rope_tables(rows, rot_dim):
    r, col = jnp.meshgrid(jnp.arange(rows), jnp.arange(GRID_W), indexing="ij")
    pos = jnp.stack([r.reshape(-1), col.reshape(-1)], axis=-1).astype(jnp.float32)
    nf = rot_dim // 4
    inv = ROPE_THETA ** (-jnp.arange(nf, dtype=jnp.float32) / nf)
    ang = pos[:, :, None] * inv
    return jnp.cos(ang), jnp.sin(ang)


def rope_2d(x, cos, sin):
    b, n, h, d = x.shape
    nf = d // 4
    xr = x.reshape(b, n, h, 2, 2, nf).astype(jnp.float32)
    x1, x2 = xr[..., 0, :], xr[..., 1, :]
    cs, sn = cos[None, :, None], sin[None, :, None]
    out = jnp.stack([x1 * cs - x2 * sn, x2 * cs + x1 * sn], axis=-2)
    return out.reshape(b, n, h, d).astype(x.dtype)


def gqa_small(q, k, v, sink=None):
    b, n, h, d = q.shape
    n_kv = k.shape[2]
    g = h // n_kv
    qg = q.reshape(b, n, n_kv, g, d)
    s = jnp.einsum("bqhgd,bkhd->bhgqk", qg, k).astype(jnp.float32) * (d ** -0.5)
    if sink is not None:
        sk = jnp.broadcast_to(sink.astype(jnp.float32).reshape(n_kv, g)[None, :, :, None, None], s.shape[:-1] + (1,))
        s = jnp.concatenate([s, sk], axis=-1)
    p = jax.nn.softmax(s, axis=-1)
    if sink is not None:
        p = p[..., :-1]
    o = jnp.einsum("bhgqk,bkhd->bqhgd", p.astype(v.dtype), v)
    return o.reshape(b, n, h, d)


def dense_gqa_blocks(q, k, v):
    b, n, h, d = q.shape
    nblk = n // Q_BLOCK
    qb = q.reshape(b, nblk, Q_BLOCK, h, d).swapaxes(0, 1)
    o = lax.map(lambda qblk: gqa_small(qblk, k, v), qb)
    return o.swapaxes(0, 1).reshape(b, n, h, d)


def window_gqa(q, k, v, k_ctx, v_ctx, sink):
    b, n, h, d = q.shape
    n_kv = k.shape[2]
    g = h // n_kv
    nblk = n // Q_BLOCK
    qb = q.reshape(b, nblk, Q_BLOCK, n_kv, g, d)
    pad = ((0, 0), (Q_BLOCK, Q_BLOCK), (0, 0), (0, 0))

    def band(t):
        tp = jnp.pad(t, pad).reshape(b, nblk + 2, Q_BLOCK, n_kv, d)
        return jnp.concatenate([tp[:, :-2], tp[:, 1:-1], tp[:, 2:]], axis=2)

    kb, vb = band(k), band(v)
    qpos = jnp.arange(n).reshape(nblk, Q_BLOCK)
    kpos = jnp.arange(nblk)[:, None] * Q_BLOCK - Q_BLOCK + jnp.arange(3 * Q_BLOCK)[None, :]
    mask = (jnp.abs(qpos[:, :, None] - kpos[:, None, :]) <= WINDOW) & (kpos[:, None, :] >= 0) & (kpos[:, None, :] < n)
    scale = d ** -0.5
    s_loc = jnp.einsum("bnqhgd,bnkhd->bnhgqk", qb, kb).astype(jnp.float32) * scale
    s_loc = jnp.where(mask[None, :, None, None], s_loc, NEG_INF)
    s_ctx = jnp.einsum("bnqhgd,bkhd->bnhgqk", qb, k_ctx).astype(jnp.float32) * scale
    s_sink = jnp.broadcast_to(sink.astype(jnp.float32).reshape(n_kv, g)[None, None, :, :, None, None], s_loc.shape[:-1] + (1,))
    p = jax.nn.softmax(jnp.concatenate([s_loc, s_ctx, s_sink], axis=-1), axis=-1)
    L = 3 * Q_BLOCK
    m = k_ctx.shape[1]
    o = (jnp.einsum("bnhgqk,bnkhd->bnqhgd", p[..., :L].astype(v.dtype), vb)
         + jnp.einsum("bnhgqk,bkhd->bnqhgd", p[..., L:L + m].astype(v.dtype), v_ctx))
    return o.reshape(b, n, h, d)


def ab_heads(hs, w_in, g_q, g_k):
    b, n, _ = hs.shape
    qa, ka, va, qb, kb, vb = jnp.split(hs @ w_in, AB_SPLITS, axis=-1)
    qa = rmsnorm(qa.reshape(b, n, N_HEADS_A, HEAD_DIM), g_q)
    ka = rmsnorm(ka.reshape(b, n, N_KV_A, HEAD_DIM), g_k)
    va = va.reshape(b, n, N_KV_A, HEAD_DIM)
    qb = qb.reshape(b, n, N_HEADS_B, HEAD_DIM)
    kb = kb.reshape(b, n, N_KV_B, HEAD_DIM)
    vb = vb.reshape(b, n, N_KV_B, HEAD_DIM)
    return qa, ka, va, qb, kb, vb


def mixer_ab(hx, hc, w_in, g_q, g_k, sink, w_out, cos, sin, need_ctx):
    qa_c, ka_c, va_c, qb_c, kb_c, vb_c = ab_heads(hc, w_in, g_q, g_k)
    qa, ka, va, qb, kb, vb = ab_heads(hx, w_in, g_q, g_k)
    qa, ka, qb, kb = rope_2d(qa, cos, sin), rope_2d(ka, cos, sin), rope_2d(qb, cos, sin), rope_2d(kb, cos, sin)
    b, n = hx.shape[0], hx.shape[1]
    oa = dense_gqa_blocks(qa, jnp.concatenate([ka_c, ka], axis=1), jnp.concatenate([va_c, va], axis=1))
    ob = window_gqa(qb, kb, vb, kb_c, vb_c, sink)
    out_x = jnp.concatenate([oa.reshape(b, n, QA), ob.reshape(b, n, QB)], axis=-1) @ w_out
    if not need_ctx:
        return out_x, None
    m = hc.shape[1]
    oa_c = gqa_small(qa_c, ka_c, va_c)
    ob_c = gqa_small(qb_c, kb_c, vb_c, sink)
    out_c = jnp.concatenate([oa_c.reshape(b, m, QA), ob_c.reshape(b, m, QB)], axis=-1) @ w_out
    return out_x, out_c


def mla_heads(hs, w_in, g_cq, g_ckv, w_uq, w_ukv):
    b, n, _ = hs.shape
    cq, ckv, k_rope = jnp.split(hs @ w_in, [MLA_Q_RANK, MLA_Q_RANK + MLA_KV_RANK], axis=-1)
    q = (rmsnorm(cq, g_cq) @ w_uq).reshape(b, n, MLA_HEADS, MLA_NOPE + MLA_ROPE)
    kv = (rmsnorm(ckv, g_ckv) @ w_ukv).reshape(b, n, MLA_HEADS, MLA_NOPE + MLA_V)
    q_nope, q_rope = q[..., :MLA_NOPE], q[..., MLA_NOPE:]
    k_nope, v = kv[..., :MLA_NOPE], kv[..., MLA_NOPE:]
    return q_nope, q_rope, k_nope, k_rope, v


def mla_attend(q_nope, q_rope, k_nope, k_rope, v):
    s = (jnp.einsum("bqhd,bkhd->bhqk", q_nope, k_nope)
         + jnp.einsum("bqhd,bkd->bhqk", q_rope, k_rope)).astype(jnp.float32) * ((MLA_NOPE + MLA_ROPE) ** -0.5)
    p = jax.nn.softmax(s, axis=-1)
    return jnp.einsum("bhqk,bkhd->bqhd", p.astype(v.dtype), v)


def mixer_mla(hx, hc, w_in, g_cq, g_ckv, w_uq, w_ukv, w_out, cos, sin, need_ctx):
    qn_c, qr_c, kn_c, kr_c, v_c = mla_heads(hc, w_in, g_cq, g_ckv, w_uq, w_ukv)
    qn, qr, kn, kr, v = mla_heads(hx, w_in, g_cq, g_ckv, w_uq, w_ukv)
    qr = rope_2d(qr, cos, sin)
    kr = rope_2d(kr[:, :, None, :], cos, sin)[:, :, 0, :]
    kn_all = jnp.concatenate([kn_c, kn], axis=1)
    kr_all = jnp.concatenate([kr_c, kr], axis=1)
    v_all = jnp.concatenate([v_c, v], axis=1)
    b, n = hx.shape[0], hx.shape[1]
    nblk = n // Q_BLOCK
    qn_b = qn.reshape(b, nblk, Q_BLOCK, MLA_HEADS, MLA_NOPE).swapaxes(0, 1)
    qr_b = qr.reshape(b, nblk, Q_BLOCK, MLA_HEADS, MLA_ROPE).swapaxes(0, 1)
    o = lax.map(lambda qs: mla_attend(qs[0], qs[1], kn_all, kr_all, v_all), (qn_b, qr_b))
    out_x = o.swapaxes(0, 1).reshape(b, n, MLA_OUT) @ w_out
    if not need_ctx:
        return out_x, None
    o_c = mla_attend(qn_c, qr_c, kn_c, kr_c, v_c)
    out_c = o_c.reshape(b, hc.shape[1], MLA_OUT) @ w_out
    return out_x, out_c


def setup_inputs(seed: int = 0) -> dict:
    key = jax.random.key(seed)
    ks = jax.random.split(key, 24)

    def nrm(k, shape, scale):
        return jax.random.normal(k, shape, jnp.float32) * scale

    return {
        "x": nrm(ks[0], (BATCH, SEQ, D_MODEL), 1.0),
        "c": nrm(ks[1], (BATCH, D_MODEL), 1.0),
        "ctx": nrm(ks[2], (BATCH, CTX_LEN, D_MODEL), 1.0),
        "c_ctx": nrm(ks[3], (D_MODEL,), 1.0),
        "w_mod": nrm(ks[4], (DEPTH, D_MODEL, N_MOD * D_MODEL), D_MODEL ** -0.5),
        "b_mod": nrm(ks[5], (DEPTH, N_MOD * D_MODEL), 0.01),
        "g_norm": 1.0 + nrm(ks[6], (DEPTH, 3, D_MODEL), 0.02),
        "w_gate_up": nrm(ks[7], (DEPTH, 2, D_MODEL, 2 * D_FF), D_MODEL ** -0.5),
        "w_down": nrm(ks[8], (DEPTH, 2, D_FF, D_MODEL), D_FF ** -0.5),
        "w_in_ab": nrm(ks[9], (N_EVEN, D_MODEL, AB_IN), D_MODEL ** -0.5),
        "g_qnorm_a": 1.0 + nrm(ks[10], (N_EVEN, HEAD_DIM), 0.02),
        "g_knorm_a": 1.0 + nrm(ks[11], (N_EVEN, HEAD_DIM), 0.02),
        "sink_b": nrm(ks[12], (N_EVEN, N_HEADS_B), 0.5),
        "w_out_ab": nrm(ks[13], (N_EVEN, AB_OUT, D_MODEL), AB_OUT ** -0.5),
        "w_in_c": nrm(ks[14], (N_ODD, D_MODEL, C_IN), D_MODEL ** -0.5),
        "g_cq": 1.0 + nrm(ks[15], (N_ODD, MLA_Q_RANK), 0.02),
        "g_ckv": 1.0 + nrm(ks[16], (N_ODD, MLA_KV_RANK), 0.02),
        "w_uq": nrm(ks[17], (N_ODD, MLA_Q_RANK, MLA_Q_OUT), MLA_Q_RANK ** -0.5),
        "w_ukv": nrm(ks[18], (N_ODD, MLA_KV_RANK, MLA_KV_OUT), MLA_KV_RANK ** -0.5),
        "w_out_c": nrm(ks[19], (N_ODD, MLA_OUT, D_MODEL), MLA_OUT ** -0.5),
        "g_final": 1.0 + nrm(ks[20], (D_MODEL,), 0.02),
    }


def reference(x, c, ctx, c_ctx, w_mod, b_mod, g_norm, w_gate_up, w_down, w_in_ab, g_qnorm_a, g_knorm_a,
              sink_b, w_out_ab, w_in_c, g_cq, g_ckv, w_uq, w_ukv, w_out_c, g_final):
    n = x.shape[1]
    rows = n // GRID_W
    cos_h, sin_h = grid_rope_tables(rows, HEAD_DIM)
    cos_m, sin_m = grid_rope_tables(rows, MLA_ROPE)
    sc = jax.nn.silu(c)
    scc = jax.nn.silu(c_ctx)
    h = ctx
    for l in range(DEPTH):
        need_ctx = l < DEPTH - 1
        mod_x = (sc @ w_mod[l] + b_mod[l]).reshape(-1, N_MOD, D_MODEL)[:, None]
        mod_c = (scc @ w_mod[l] + b_mod[l]).reshape(N_MOD, D_MODEL)
        x = ffn_sub(x, mod_x, 0, g_norm[l, 0], w_gate_up[l, 0], w_down[l, 0])
        h = ffn_sub(h, mod_c, 0, g_norm[l, 0], w_gate_up[l, 0], w_down[l, 0])
        ax = adaln_in(x, g_norm[l, 1], mod_x, 1)
        ah = adaln_in(h, g_norm[l, 1], mod_c, 1)
        i = l // 2
        if l % 2 == 0:
            ox, oh = mixer_ab(ax, ah, w_in_ab[i], g_qnorm_a[i], g_knorm_a[i], sink_b[i], w_out_ab[i], cos_h, sin_h, need_ctx)
        else:
            ox, oh = mixer_mla(ax, ah, w_in_c[i], g_cq[i], g_ckv[i], w_uq[i], w_ukv[i], w_out_c[i], cos_m, sin_m, need_ctx)
        x = x + mod_x[..., 5, :] * ox
        x = ffn_sub(x, mod_x, 2, g_norm[l, 2], w_gate_up[l, 1], w_down[l, 1])
        if need_ctx:
            h = h + mod_c[..., 5, :] * oh
            h = ffn_sub(h, mod_c, 2, g_norm[l, 2], w_gate_up[l, 1], w_down[l, 1])
    return rmsnorm(x, g_final)
```

```python
import functools

import jax
import jax.numpy as jnp
from jax import lax
from jax.experimental import pallas as pl
from jax.experimental.pallas import tpu as pltpu

GRID_W = 64
HEAD_DIM = 128
N_HEADS_A = 8
N_KV_A = 2
N_HEADS_B = 8
N_KV_B = 2
WINDOW = 128
Q_BLOCK = 128
ROPE_THETA = 10000.0
MLA_HEADS = 16
MLA_Q_RANK = 512
MLA_KV_RANK = 256
MLA_NOPE = 128
MLA_ROPE = 64
MLA_V = 128
MLA_QK = 256
N_MOD = 9
EPS = 1e-6
NEG_INF = -1e30

QA = N_HEADS_A * HEAD_DIM
KA = N_KV_A * HEAD_DIM
QB = N_HEADS_B * HEAD_DIM
KB = N_KV_B * HEAD_DIM
AB_IN = QA + 2 * KA + QB + 2 * KB
GROUP_A = N_HEADS_A // N_KV_A
GROUP_B = N_HEADS_B // N_KV_B
COL_QA = 0
COL_KA = N_HEADS_A
COL_VA = COL_KA + N_KV_A
COL_QB = COL_VA + N_KV_A
COL_KB = COL_QB + N_HEADS_B
COL_VB = COL_KB + N_KV_B
N_COLS_AB = COL_VB + N_KV_B
C_IN_PAD = MLA_Q_RANK + MLA_KV_RANK + 128

F32 = jnp.float32
BF16 = jnp.bfloat16
V7X_VMEM_LIMIT_BYTES = 56 * 1024 * 1024


def _params(*sem):
    return pltpu.CompilerParams(dimension_semantics=sem, vmem_limit_bytes=V7X_VMEM_LIMIT_BYTES)


def _rms(x):
    return x * lax.rsqrt(jnp.mean(x * x, axis=-1, keepdims=True) + EPS)


def _silu(x):
    return x / (1.0 + jnp.exp(-x))


def _mod_row(mod_ref, k):
    return mod_ref[0, 0, k:k + 1, :]


def _adaln(x, g_ref, jn, mod_ref, j):
    return _rms(x) * g_ref[0, jn:jn + 1, :] * (1.0 + _mod_row(mod_ref, 3 * j + 1)) + _mod_row(mod_ref, 3 * j)


def _mod_kernel(c_ref, w_ref, b_ref, o_ref):
    sc = _silu(c_ref[...]).astype(BF16)
    o_ref[0] = jnp.dot(sc, w_ref[0].astype(BF16), preferred_element_type=F32) + b_ref[0]


def _mod_call(cc, w_mod, b_mod):
    depth, d, nd = w_mod.shape
    tn = min(d, 1024)
    out = pl.pallas_call(
        _mod_kernel,
        grid=(depth, nd // tn),
        in_specs=[
            pl.BlockSpec((8, d), lambda l, j: (0, 0)),
            pl.BlockSpec((1, d, tn), lambda l, j: (l, 0, j)),
            pl.BlockSpec((1, 1, tn), lambda l, j: (l, 0, j)),
        ],
        out_specs=pl.BlockSpec((1, 8, tn), lambda l, j: (l, 0, j)),
        out_shape=jax.ShapeDtypeStruct((depth, 8, nd), F32),
        compiler_params=_params("parallel", "parallel"),
        name="mod",
    )(cc, w_mod, b_mod.reshape(depth, 1, nd))
    return out.reshape(depth, 8, N_MOD, d)


def _ffn_kernel(*refs, jm, jn, final):
    if final:
        x_ref, mod_ref, g_ref, wg_ref, wu_ref, wd_ref, gf_ref, o_ref, xn_ref = refs
    else:
        x_ref, mod_ref, g_ref, wg_ref, wu_ref, wd_ref, o_ref, xn_ref = refs
    j = pl.program_id(1)

    @pl.when(j == 0)
    def _():
        xn_ref[...] = _adaln(x_ref[...], g_ref, jn, mod_ref, jm).astype(BF16)

    xn = xn_ref[...]
    gt = jnp.dot(xn, wg_ref[0, 0], preferred_element_type=F32)
    up = jnp.dot(xn, wu_ref[0, 0], preferred_element_type=F32)
    h = (_silu(gt) * up).astype(BF16)
    part = jnp.dot(h, wd_ref[0, 0], preferred_element_type=F32)

    @pl.when(j == 0)
    def _():
        o_ref[...] = part

    @pl.when(j > 0)
    def _():
        o_ref[...] += part

    @pl.when(j == pl.num_programs(1) - 1)
    def _():
        y = x_ref[...] + (0.5 * _mod_row(mod_ref, 3 * jm + 2)) * o_ref[...]
        if final:
            y = _rms(y) * gf_ref[...]
        o_ref[...] = y


def _ffn_call(x, mod4, g_norm, w_gu, w_dn, *, l, jf, jm, mod_base, rows_per_mod, g_final=None):
    t, d = x.shape
    f = w_dn.shape[2]
    tm = min(512, t)
    tf = min(512, f)
    nf = f // tf
    final = g_final is not None
    in_specs = [
        pl.BlockSpec((tm, d), lambda i, j: (i, 0)),
        pl.BlockSpec((1, 1, N_MOD, d), lambda i, j: (l, mod_base + (i * tm) // rows_per_mod, 0, 0)),
        pl.BlockSpec((1, 3, d), lambda i, j: (l, 0, 0)),
        pl.BlockSpec((1, 1, d, tf), lambda i, j: (l, jf, 0, j)),
        pl.BlockSpec((1, 1, d, tf), lambda i, j: (l, jf, 0, nf + j)),
        pl.BlockSpec((1, 1, tf, d), lambda i, j: (l, jf, j, 0)),
    ]
    args = [x, mod4, g_norm, w_gu, w_gu, w_dn]
    if final:
        in_specs.append(pl.BlockSpec((1, d), lambda i, j: (0, 0)))
        args.append(g_final.reshape(1, d))
    return pl.pallas_call(
        functools.partial(_ffn_kernel, jm=jm, jn=jm, final=final),
        grid=(t // tm, nf),
        in_specs=in_specs,
        out_specs=pl.BlockSpec((tm, d), lambda i, j: (i, 0)),
        out_shape=jax.ShapeDtypeStruct((t, d), F32),
        scratch_shapes=[pltpu.VMEM((tm, d), BF16)],
        compiler_params=_params("parallel", "arbitrary"),
        name="ffn",
    )(*args)


def _rope_tables(n_pos, rot_dim):
    nf = rot_dim // 4
    pos = jnp.arange(n_pos)
    inv = ROPE_THETA ** (-jnp.arange(nf, dtype=F32) / nf)
    ang_r = (pos // GRID_W).astype(F32)[:, None] * inv
    ang_c = (pos % GRID_W).astype(F32)[:, None] * inv
    cos = jnp.concatenate([jnp.cos(ang_r), jnp.cos(ang_r), jnp.cos(ang_c), jnp.cos(ang_c)], axis=-1)
    sin = jnp.concatenate([-jnp.sin(ang_r), jnp.sin(ang_r), -jnp.sin(ang_c), jnp.sin(ang_c)], axis=-1)
    return cos, sin


def _rotate(y, cos, sin, nf):
    lanes = y.shape[-1]
    lane = lax.broadcasted_iota(jnp.int32, y.shape, 1)
    partner = jnp.where(lane % (2 * nf) < nf, pltpu.roll(y, lanes - nf, 1), pltpu.roll(y, nf, 1))
    return y * cos + partner * sin


def _proj_ab_kernel(*refs, rope):
    if rope:
        x_ref, mod_ref, g_ref, w_ref, gq_ref, gk_ref, cos_ref, sin_ref, o_ref = refs
        cos, sin = cos_ref[...], sin_ref[...]
    else:
        x_ref, mod_ref, g_ref, w_ref, gq_ref, gk_ref, o_ref = refs
    xn = _adaln(x_ref[...], g_ref, 1, mod_ref, 1).astype(BF16)
    scale = HEAD_DIM ** -0.5
    for c in range(N_COLS_AB // 2):
        y2 = jnp.dot(xn, w_ref[:, c * 256:(c + 1) * 256], preferred_element_type=F32)
        for hh in range(2):
            col = 2 * c + hh
            y = y2[:, hh * HEAD_DIM:(hh + 1) * HEAD_DIM]
            is_qa = col < COL_KA
            is_ka = COL_KA <= col < COL_VA
            is_qb = COL_QB <= col < COL_KB
            is_kb = COL_KB <= col < COL_VB
            if is_qa:
                y = _rms(y) * gq_ref[...]
            if is_ka:
                y = _rms(y) * gk_ref[...]
            if rope and (is_qa or is_ka or is_qb or is_kb):
                y = _rotate(y, cos, sin, HEAD_DIM // 4)
            if is_qa or is_qb:
                y = y * scale
            o_ref[:, col * HEAD_DIM:(col + 1) * HEAD_DIM] = y.astype(BF16)


def _proj_ab_call(x, mod4, g_norm, w_in, g_q, g_k, tables, *, l, mod_base, rows_per_mod, n_pos):
    t, d = x.shape
    tm = min(256, t)
    rope = tables is not None
    const = dict(pipeline_mode=pl.Buffered(1))
    in_specs = [
        pl.BlockSpec((tm, d), lambda i: (i, 0)),
        pl.BlockSpec((1, 1, N_MOD, d), lambda i: (l, mod_base + (i * tm) // rows_per_mod, 0, 0)),
        pl.BlockSpec((1, 3, d), lambda i: (l, 0, 0)),
        pl.BlockSpec((d, AB_IN), lambda i: (0, 0), **const),
        pl.BlockSpec((1, HEAD_DIM), lambda i: (0, 0)),
        pl.BlockSpec((1, HEAD_DIM), lambda i: (0, 0)),
    ]
    args = [x, mod4, g_norm, w_in, g_q.reshape(1, HEAD_DIM), g_k.reshape(1, HEAD_DIM)]
    if rope:
        npb = n_pos // tm
        in_specs += [pl.BlockSpec((tm, HEAD_DIM), lambda i: (i % npb, 0))] * 2
        args += list(tables)
    return pl.pallas_call(
        functools.partial(_proj_ab_kernel, rope=rope),
        grid=(t // tm,),
        in_specs=in_specs,
        out_specs=pl.BlockSpec((tm, AB_IN), lambda i: (i, 0)),
        out_shape=jax.ShapeDtypeStruct((t, AB_IN), BF16),
        compiler_params=_params("parallel"),
        name="proj_ab",
    )(*args)


def _proj_mla_kernel(*refs, rope):
    if rope:
        x_ref, mod_ref, g_ref, wi_ref, gcq_ref, gckv_ref, wuq_ref, wukv_ref, cos_ref, sin_ref, q_ref, k_ref, v_ref = refs
        cos, sin = cos_ref[...], sin_ref[...]
    else:
        x_ref, mod_ref, g_ref, wi_ref, gcq_ref, gckv_ref, wuq_ref, wukv_ref, q_ref, k_ref, v_ref = refs
    xn = _adaln(x_ref[...], g_ref, 1, mod_ref, 1).astype(BF16)
    t = jnp.dot(xn, wi_ref[...], preferred_element_type=F32)
    cq = (_rms(t[:, :MLA_Q_RANK]) * gcq_ref[...]).astype(BF16)
    ckv = (_rms(t[:, MLA_Q_RANK:MLA_Q_RANK + MLA_KV_RANK]) * gckv_ref[...]).astype(BF16)
    kr = t[:, MLA_Q_RANK + MLA_KV_RANK:]
    if rope:
        kr = _rotate(kr, cos, sin, MLA_ROPE // 4)
    kr = kr.astype(BF16)
    scale = (MLA_NOPE + MLA_ROPE) ** -0.5
    for h in range(MLA_HEADS):
        q = jnp.dot(cq, wuq_ref[:, h * MLA_QK:(h + 1) * MLA_QK], preferred_element_type=F32)
        qn, qr = q[:, :MLA_NOPE], q[:, MLA_NOPE:]
        if rope:
            qr = _rotate(qr, cos, sin, MLA_ROPE // 4)
        q_ref[:, h * MLA_QK:h * MLA_QK + MLA_NOPE] = (qn * scale).astype(BF16)
        q_ref[:, h * MLA_QK + MLA_NOPE:(h + 1) * MLA_QK] = (qr * scale).astype(BF16)
        kv = jnp.dot(ckv, wukv_ref[:, h * 256:(h + 1) * 256], preferred_element_type=F32)
        k_ref[:, h * MLA_QK:h * MLA_QK + MLA_NOPE] = kv[:, :MLA_NOPE].astype(BF16)
        k_ref[:, h * MLA_QK + MLA_NOPE:(h + 1) * MLA_QK] = kr
        v_ref[:, h * MLA_V:(h + 1) * MLA_V] = kv[:, MLA_NOPE:].astype(BF16)


def _proj_mla_call(x, mod4, g_norm, w_in, g_cq, g_ckv, w_uq, w_ukv, tables, *, l, mod_base, rows_per_mod, n_pos):
    t, d = x.shape
    tm = min(256, t)
    rope = tables is not None
    const = dict(pipeline_mode=pl.Buffered(1))
    in_specs = [
        pl.BlockSpec((tm, d), lambda i: (i, 0)),
        pl.BlockSpec((1, 1, N_MOD, d), lambda i: (l, mod_base + (i * tm) // rows_per_mod, 0, 0)),
        pl.BlockSpec((1, 3, d), lambda i: (l, 0, 0)),
        pl.BlockSpec((d, C_IN_PAD), lambda i: (0, 0), **const),
        pl.BlockSpec((1, MLA_Q_RANK), lambda i: (0, 0)),
        pl.BlockSpec((1, MLA_KV_RANK), lambda i: (0, 0)),
        pl.BlockSpec((MLA_Q_RANK, MLA_HEADS * MLA_QK), lambda i: (0, 0), **const),
        pl.BlockSpec((MLA_KV_RANK, MLA_HEADS * 256), lambda i: (0, 0), **const),
    ]
    args = [x, mod4, g_norm, w_in, g_cq.reshape(1, -1), g_ckv.reshape(1, -1), w_uq, w_ukv]
    if rope:
        npb = n_pos // tm
        in_specs += [pl.BlockSpec((tm, 128), lambda i: (i % npb, 0))] * 2
        args += list(tables)
    return pl.pallas_call(
        functools.partial(_proj_mla_kernel, rope=rope),
        grid=(t // tm,),
        in_specs=in_specs,
        out_specs=[
            pl.BlockSpec((tm, MLA_HEADS * MLA_QK), lambda i: (i, 0)),
            pl.BlockSpec((tm, MLA_HEADS * MLA_QK), lambda i: (i, 0)),
            pl.BlockSpec((tm, MLA_HEADS * MLA_V), lambda i: (i, 0)),
        ],
        out_shape=[
            jax.ShapeDtypeStruct((t, MLA_HEADS * MLA_QK), BF16),
            jax.ShapeDtypeStruct((t, MLA_HEADS * MLA_QK), BF16),
            jax.ShapeDtypeStruct((t, MLA_HEADS * MLA_V), BF16),
        ],
        compiler_params=_params("parallel"),
        name="proj_mla",
    )(*args)


def _nt_dot(a, b):
    return lax.dot_general(a, b, (((1,), (1,)), ((), ())), preferred_element_type=F32)


def _flash_kernel(*refs, group, dq, dv, tk, n_lat, has_sink):
    refs = list(refs)
    q_ref, kc_ref, vc_ref = refs[:3]
    pos = 3
    if n_lat:
        kl_ref, vl_ref = refs[pos:pos + 2]
        pos += 2
    if has_sink:
        sink_ref = refs[pos]
        pos += 1
    o_ref = refs[pos]
    tq = q_ref.shape[0]
    q = jnp.concatenate([q_ref[:, g * dq:(g + 1) * dq] for g in range(group)], axis=0) if group > 1 else q_ref[...]

    s = _nt_dot(q, kc_ref[...])
    m = jnp.max(s, axis=-1, keepdims=True)
    if has_sink:
        head0 = pl.program_id(1) * group
        sink = jnp.concatenate(
            [jnp.full((tq, 1), sink_ref[head0 + g], F32) for g in range(group)], axis=0)
        m = jnp.maximum(m, sink)
    p = jnp.exp(s - m)
    l = jnp.sum(p, axis=-1, keepdims=True)
    if has_sink:
        l = l + jnp.exp(sink - m)
    acc = jnp.dot(p.astype(BF16), vc_ref[...], preferred_element_type=F32)

    if n_lat:
        def body(t, carry):
            m, l, acc = carry
            r0 = pl.multiple_of(t * tk, tk)
            s = _nt_dot(q, kl_ref[pl.ds(r0, tk), :])
            m_new = jnp.maximum(m, jnp.max(s, axis=-1, keepdims=True))
            alpha = jnp.exp(m - m_new)
            p = jnp.exp(s - m_new)
            l = alpha * l + jnp.sum(p, axis=-1, keepdims=True)
            acc = alpha * acc + jnp.dot(p.astype(BF16), vl_ref[pl.ds(r0, tk), :], preferred_element_type=F32)
            return m_new, l, acc

        m, l, acc = lax.fori_loop(0, n_lat, body, (m, l, acc))

    o = (acc / l).astype(BF16)
    for g in range(group):
        o_ref[:, g * dv:(g + 1) * dv] = o[g * tq:(g + 1) * tq]


def _flash_call(q_arr, kv_ctx, kv_lat, sink, *, batch, n_q, n_ctx, n_heads, group, dq, dv, q_col, k_col, v_col, tq):
    n_kv = n_heads // group
    nqb = n_q // tq
    has_sink = sink is not None
    in_specs = [
        pl.BlockSpec((tq, group * dq), lambda b, g, i: (b * nqb + i, q_col // group + g)),
        pl.BlockSpec((n_ctx, dq), lambda b, g, i: (b, k_col + g)),
        pl.BlockSpec((n_ctx, dv), lambda b, g, i: (b, v_col + g)),
    ]
    args = [q_arr, kv_ctx[0], kv_ctx[1]]
    n_lat = 0
    tk = 0
    if kv_lat is not None:
        n_rows = kv_lat[0].shape[0] // batch
        tk = min(512, n_rows)
        n_lat = n_rows // tk
        in_specs += [
            pl.BlockSpec((n_rows, dq), lambda b, g, i: (b, k_col + g)),
            pl.BlockSpec((n_rows, dv), lambda b, g, i: (b, v_col + g)),
        ]
        args += [kv_lat[0], kv_lat[1]]
    if has_sink:
        in_specs.append(pl.BlockSpec(memory_space=pltpu.SMEM))
        args.append(sink)
    return pl.pallas_call(
        functools.partial(_flash_kernel, group=group, dq=dq, dv=dv, tk=tk, n_lat=n_lat, has_sink=has_sink),
        grid=(batch, n_kv, nqb),
        in_specs=in_specs,
        out_specs=pl.BlockSpec((tq, group * dv), lambda b, g, i: (b * nqb + i, g)),
        out_shape=jax.ShapeDtypeStruct((batch * n_q, n_heads * dv), BF16),
        compiler_params=_params("parallel", "parallel", "arbitrary"),
        name="flash",
    )(*args)


def _window_kernel(q_ref, kp_ref, kc_ref, kn_ref, vp_ref, vc_ref, vn_ref, kx_ref, vx_ref, sink_ref, o_ref):
    i = pl.program_id(2)
    nblk = pl.num_programs(2)
    qb = Q_BLOCK
    q = jnp.concatenate([q_ref[:, g * HEAD_DIM:(g + 1) * HEAD_DIM] for g in range(GROUP_B)], axis=0)
    rows = GROUP_B * qb
    r = lax.broadcasted_iota(jnp.int32, (rows, qb), 0) % qb
    c = lax.broadcasted_iota(jnp.int32, (rows, qb), 1)
    off_p = jnp.where(i > 0, 0, qb)
    off_n = jnp.where(i < nblk - 1, 0, qb)
    s_p = jnp.where(c >= r + off_p, _nt_dot(q, kp_ref[...]), NEG_INF)
    s_c = _nt_dot(q, kc_ref[...])
    s_n = jnp.where(c + off_n <= r, _nt_dot(q, kn_ref[...]), NEG_INF)
    s_x = _nt_dot(q, kx_ref[...])
    head0 = pl.program_id(1) * GROUP_B
    sink = jnp.concatenate([jnp.full((qb, 1), sink_ref[head0 + g], F32) for g in range(GROUP_B)], axis=0)
    m = jnp.maximum(jnp.maximum(jnp.max(s_p, -1, keepdims=True), jnp.max(s_c, -1, keepdims=True)),
                    jnp.maximum(jnp.max(s_n, -1, keepdims=True), jnp.max(s_x, -1, keepdims=True)))
    m = jnp.maximum(m, sink)
    p_p, p_c, p_n, p_x = jnp.exp(s_p - m), jnp.exp(s_c - m), jnp.exp(s_n - m), jnp.exp(s_x - m)
    l = (jnp.sum(p_p, -1, keepdims=True) + jnp.sum(p_c, -1, keepdims=True)
         + jnp.sum(p_n, -1, keepdims=True) + jnp.sum(p_x, -1, keepdims=True) + jnp.exp(sink - m))
    acc = (jnp.dot(p_p.astype(BF16), vp_ref[...], preferred_element_type=F32)
           + jnp.dot(p_c.astype(BF16), vc_ref[...], preferred_element_type=F32)
           + jnp.dot(p_n.astype(BF16), vn_ref[...], preferred_element_type=F32)
           + jnp.dot(p_x.astype(BF16), vx_ref[...], preferred_element_type=F32))
    o = (acc / l).astype(BF16)
    for g in range(GROUP_B):
        o_ref[:, g * HEAD_DIM:(g + 1) * HEAD_DIM] = o[g * qb:(g + 1) * qb]


def _window_call(qkv_x, qkv_c, sink, *, batch, n, n_ctx):
    assert WINDOW == Q_BLOCK
    nblk = n // Q_BLOCK
    qb = Q_BLOCK

    def band(col, shift):
        def index(b, g, i):
            return (b * nblk + jnp.clip(i + shift, 0, nblk - 1), col + g)
        return pl.BlockSpec((qb, HEAD_DIM), index)

    in_specs = [
        pl.BlockSpec((qb, GROUP_B * HEAD_DIM), lambda b, g, i: (b * nblk + i, COL_QB // GROUP_B + g)),
        band(COL_KB, -1), band(COL_KB, 0), band(COL_KB, 1),
        band(COL_VB, -1), band(COL_VB, 0), band(COL_VB, 1),
        pl.BlockSpec((n_ctx, HEAD_DIM), lambda b, g, i: (b, COL_KB + g)),
        pl.BlockSpec((n_ctx, HEAD_DIM), lambda b, g, i: (b, COL_VB + g)),
        pl.BlockSpec(memory_space=pltpu.SMEM),
    ]
    return pl.pallas_call(
        _window_kernel,
        grid=(batch, N_KV_B, nblk),
        in_specs=in_specs,
        out_specs=pl.BlockSpec((qb, GROUP_B * HEAD_DIM), lambda b, g, i: (b * nblk + i, g)),
        out_shape=jax.ShapeDtypeStruct((batch * n, QB), BF16),
        compiler_params=_params("parallel", "parallel", "arbitrary"),
        name="window",
    )(qkv_x, qkv_x, qkv_x, qkv_x, qkv_x, qkv_x, qkv_x, qkv_c, qkv_c, sink)


def _out_kernel(*refs, n_in):
    x_ref, mod_ref = refs[:2]
    o_ref = refs[2 + 2 * n_in]
    y = None
    for k in range(n_in):
        part = jnp.dot(refs[2 + 2 * k][...], refs[3 + 2 * k][...], preferred_element_type=F32)
        y = part if y is None else y + part
    o_ref[...] = x_ref[...] + _mod_row(mod_ref, 5) * y


def _out_call(x, mod4, pairs, *, l, mod_base, rows_per_mod):
    t, d = x.shape
    tm = min(512, t)
    in_specs = [
        pl.BlockSpec((tm, d), lambda i: (i, 0)),
        pl.BlockSpec((1, 1, N_MOD, d), lambda i: (l, mod_base + (i * tm) // rows_per_mod, 0, 0)),
    ]
    args = [x, mod4]
    for o, w in pairs:
        kk = o.shape[1]
        in_specs += [
            pl.BlockSpec((tm, kk), lambda i: (i, 0)),
            pl.BlockSpec((kk, d), lambda i: (0, 0), pipeline_mode=pl.Buffered(1)),
        ]
        args += [o, w]
    return pl.pallas_call(
        functools.partial(_out_kernel, n_in=len(pairs)),
        grid=(t // tm,),
        in_specs=in_specs,
        out_specs=pl.BlockSpec((tm, d), lambda i: (i, 0)),
        out_shape=jax.ShapeDtypeStruct((t, d), F32),
        compiler_params=_params("parallel"),
        name="out_proj",
    )(*args)


def _mla_weight_layout(w_in_c, w_uq):
    d = w_in_c.shape[0]
    w_in = jnp.concatenate([w_in_c, jnp.zeros((d, C_IN_PAD - w_in_c.shape[1]), w_in_c.dtype)], axis=1)
    wq = w_uq.reshape(MLA_Q_RANK, MLA_HEADS, MLA_NOPE + MLA_ROPE)
    wq = jnp.concatenate([wq, jnp.zeros((MLA_Q_RANK, MLA_HEADS, MLA_QK - MLA_NOPE - MLA_ROPE), w_uq.dtype)], axis=-1)
    return w_in.astype(BF16), wq.reshape(MLA_Q_RANK, MLA_HEADS * MLA_QK).astype(BF16)


def kernel(x, c, ctx, c_ctx, w_mod, b_mod, g_norm, w_gate_up, w_down, w_in_ab, g_qnorm_a, g_knorm_a,
           sink_b, w_out_ab, w_in_c, g_cq, g_ckv, w_uq, w_ukv, w_out_c, g_final):
    batch, n, d = x.shape
    n_ctx = ctx.shape[1]
    depth = w_mod.shape[0]
    assert batch + 1 <= 8 and n % GRID_W == 0

    cc = jnp.concatenate([c, c_ctx[None], jnp.zeros((8 - batch - 1, d), F32)], axis=0)
    mod4 = _mod_call(cc, w_mod, b_mod)

    w_gu = w_gate_up.astype(BF16)
    w_dn = w_down.astype(BF16)
    w_ab = w_in_ab.astype(BF16)
    w_oab = w_out_ab.astype(BF16)
    w_oc = w_out_c.astype(BF16)
    w_kv = w_ukv.astype(BF16)

    cos_h, sin_h = _rope_tables(n, HEAD_DIM)
    cos_m, sin_m = _rope_tables(n, MLA_ROPE)
    pad = jnp.zeros((n, 128 - MLA_ROPE), F32)
    tables_h = (cos_h, sin_h)
    tables_m = (jnp.concatenate([cos_m, pad], axis=1), jnp.concatenate([sin_m, pad], axis=1))

    xs = x.reshape(batch * n, d)
    hs = ctx.reshape(batch * n_ctx, d)
    x_mod = dict(mod_base=0, rows_per_mod=n)
    c_mod = dict(mod_base=batch, rows_per_mod=batch * n_ctx)

    for l in range(depth):
        need_ctx = l < depth - 1
        last = l == depth - 1
        i = l // 2
        xs = _ffn_call(xs, mod4, g_norm, w_gu, w_dn, l=l, jf=0, jm=0, **x_mod)
        hs = _ffn_call(hs, mod4, g_norm, w_gu, w_dn, l=l, jf=0, jm=0, **c_mod)
        if l % 2 == 0:
            qkv_x = _proj_ab_call(xs, mod4, g_norm, w_ab[i], g_qnorm_a[i], g_knorm_a[i], tables_h,
                                  l=l, n_pos=n, **x_mod)
            qkv_c = _proj_ab_call(hs, mod4, g_norm, w_ab[i], g_qnorm_a[i], g_knorm_a[i], None,
                                  l=l, n_pos=n, **c_mod)
            oa = _flash_call(qkv_x, (qkv_c, qkv_c), (qkv_x, qkv_x), None, batch=batch, n_q=n, n_ctx=n_ctx,
                             n_heads=N_HEADS_A, group=GROUP_A, dq=HEAD_DIM, dv=HEAD_DIM,
                             q_col=COL_QA, k_col=COL_KA, v_col=COL_VA, tq=min(128, n))
            ob = _window_call(qkv_x, qkv_c, sink_b[i], batch=batch, n=n, n_ctx=n_ctx)
            w_o = w_oab[i]
            xs_new = _out_call(xs, mod4, [(oa, w_o[:QA]), (ob, w_o[QA:])], l=l, **x_mod)
            if need_ctx:
                oa_c = _flash_call(qkv_c, (qkv_c, qkv_c), None, None, batch=batch, n_q=n_ctx, n_ctx=n_ctx,
                                   n_heads=N_HEADS_A, group=GROUP_A, dq=HEAD_DIM, dv=HEAD_DIM,
                                   q_col=COL_QA, k_col=COL_KA, v_col=COL_VA, tq=min(128, n_ctx))
                ob_c = _flash_call(qkv_c, (qkv_c, qkv_c), None, sink_b[i], batch=batch, n_q=n_ctx, n_ctx=n_ctx,
                                   n_heads=N_HEADS_B, group=GROUP_B, dq=HEAD_DIM, dv=HEAD_DIM,
                                   q_col=COL_QB, k_col=COL_KB, v_col=COL_VB, tq=min(128, n_ctx))
                hs = _out_call(hs, mod4, [(oa_c, w_o[:QA]), (ob_c, w_o[QA:])], l=l, **c_mod)
            xs = xs_new
        else:
            w_in, wq = _mla_weight_layout(w_in_c[i], w_uq[i])
            q_x, k_x, v_x = _proj_mla_call(xs, mod4, g_norm, w_in, g_cq[i], g_ckv[i], wq, w_kv[i], tables_m,
                                           l=l, n_pos=n, **x_mod)
            q_c, k_c, v_c = _proj_mla_call(hs, mod4, g_norm, w_in, g_cq[i], g_ckv[i], wq, w_kv[i], None,
                                           l=l, n_pos=n, **c_mod)
            o = _flash_call(q_x, (k_c, v_c), (k_x, v_x), None, batch=batch, n_q=n, n_ctx=n_ctx,
                            n_heads=MLA_HEADS, group=1, dq=MLA_QK, dv=MLA_V, q_col=0, k_col=0, v_col=0,
                            tq=min(512, n))
            xs_new = _out_call(xs, mod4, [(o, w_oc[i])], l=l, **x_mod)
            if need_ctx:
                o_c = _flash_call(q_c, (k_c, v_c), None, None, batch=batch, n_q=n_ctx, n_ctx=n_ctx,
                                  n_heads=MLA_HEADS, group=1, dq=MLA_QK, dv=MLA_V, q_col=0, k_col=0, v_col=0,
                                  tq=n_ctx)
                hs = _out_call(hs, mod4, [(o_c, w_oc[i])], l=l, **c_mod)
            xs = xs_new
        xs = _ffn_call(xs, mod4, g_norm, w_gu, w_dn, l=l, jf=1, jm=2, g_final=g_final if last else None, **x_mod)
        if need_ctx:
            hs = _ffn_call(hs, mod4, g_norm, w_gu, w_dn, l=l, jf=1, jm=2, **c_mod)
    return xs.reshape(batch, n, d)
```

```python
import functools

import jax
import jax.numpy as jnp
from jax import lax
from jax.experimental import pallas as pl
from jax.experimental.pallas import tpu as pltpu

GRID_W = 64
HEAD_DIM = 128
N_HEADS_A = 8
N_KV_A = 2
N_HEADS_B = 8
N_KV_B = 2
WINDOW = 128
Q_BLOCK = 128
ROPE_THETA = 10000.0
MLA_HEADS = 16
MLA_Q_RANK = 512
MLA_KV_RANK = 256
MLA_NOPE = 128
MLA_ROPE = 64
MLA_V = 128
MLA_QK = 256
N_MOD = 9
EPS = 1e-6
NEG_INF = -1e30
LOG2E = 1.4426950408889634

QA = N_HEADS_A * HEAD_DIM
KA = N_KV_A * HEAD_DIM
QB = N_HEADS_B * HEAD_DIM
KB = N_KV_B * HEAD_DIM
AB_IN = QA + 2 * KA + QB + 2 * KB
GROUP_A = N_HEADS_A // N_KV_A
GROUP_B = N_HEADS_B // N_KV_B
COL_QA = 0
COL_KA = N_HEADS_A
COL_VA = COL_KA + N_KV_A
COL_QB = COL_VA + N_KV_A
COL_KB = COL_QB + N_HEADS_B
COL_VB = COL_KB + N_KV_B
N_COLS_AB = COL_VB + N_KV_B
C_IN_PAD = MLA_Q_RANK + MLA_KV_RANK + 128

F32 = jnp.float32
BF16 = jnp.bfloat16
V7X_VMEM_LIMIT_BYTES = 56 * 1024 * 1024
FLASH_UNROLL = 2


def _params(*sem):
    return pltpu.CompilerParams(dimension_semantics=sem, vmem_limit_bytes=V7X_VMEM_LIMIT_BYTES)


def _rms(x):
    return x * lax.rsqrt(jnp.mean(x * x, axis=-1, keepdims=True) + EPS)


def _silu(x):
    return x / (1.0 + jnp.exp(-x))


def _mod_row(mod_ref, k):
    return mod_ref[0, 0, k:k + 1, :]


def _adaln(x, g_ref, jn, mod_ref, j):
    return _rms(x) * g_ref[0, jn:jn + 1, :] * (1.0 + _mod_row(mod_ref, 3 * j + 1)) + _mod_row(mod_ref, 3 * j)


def _mod_kernel(c_ref, w_ref, b_ref, o_ref):
    sc = _silu(c_ref[...]).astype(BF16)
    o_ref[0] = jnp.dot(sc, w_ref[0].astype(BF16), preferred_element_type=F32) + b_ref[0]


def _mod_call(cc, w_mod, b_mod):
    depth, d, nd = w_mod.shape
    tn = min(d, 1024)
    out = pl.pallas_call(
        _mod_kernel,
        grid=(depth, nd // tn),
        in_specs=[
            pl.BlockSpec((8, d), lambda l, j: (0, 0)),
            pl.BlockSpec((1, d, tn), lambda l, j: (l, 0, j)),
            pl.BlockSpec((1, 1, tn), lambda l, j: (l, 0, j)),
        ],
        out_specs=pl.BlockSpec((1, 8, tn), lambda l, j: (l, 0, j)),
        out_shape=jax.ShapeDtypeStruct((depth, 8, nd), F32),
        compiler_params=_params("parallel", "parallel"),
        name="mod",
    )(cc, w_mod, b_mod.reshape(depth, 1, nd))
    return out.reshape(depth, 8, N_MOD, d)


def _ffn_kernel(*refs, jm, jn, final):
    if final:
        x_ref, mod_ref, g_ref, wg_ref, wu_ref, wd_ref, gf_ref, o_ref, xn_ref = refs
    else:
        x_ref, mod_ref, g_ref, wg_ref, wu_ref, wd_ref, o_ref, xn_ref = refs
    j = pl.program_id(1)

    @pl.when(j == 0)
    def _():
        xn_ref[...] = _adaln(x_ref[...], g_ref, jn, mod_ref, jm).astype(BF16)
        o_ref[...] = jnp.zeros_like(o_ref)

    xn = xn_ref[...]
    gt = jnp.dot(xn, wg_ref[0, 0], preferred_element_type=F32)
    up = jnp.dot(xn, wu_ref[0, 0], preferred_element_type=F32)
    h = (_silu(gt) * up).astype(BF16)
    o_ref[...] += jnp.dot(h, wd_ref[0, 0], preferred_element_type=F32)

    @pl.when(j == pl.num_programs(1) - 1)
    def _():
        y = x_ref[...] + (0.5 * _mod_row(mod_ref, 3 * jm + 2)) * o_ref[...]
        if final:
            y = _rms(y) * gf_ref[...]
        o_ref[...] = y


def _ffn_call(x, mod4, g_norm, w_gu, w_dn, *, l, jf, jm, mod_base, rows_per_mod, g_final=None):
    t, d = x.shape
    f = w_dn.shape[2]
    tm = min(512, t)
    tf = min(512, f)
    nf = f // tf
    final = g_final is not None
    in_specs = [
        pl.BlockSpec((tm, d), lambda i, j: (i, 0)),
        pl.BlockSpec((1, 1, N_MOD, d), lambda i, j: (l, mod_base + (i * tm) // rows_per_mod, 0, 0)),
        pl.BlockSpec((1, 3, d), lambda i, j: (l, 0, 0)),
        pl.BlockSpec((1, 1, d, tf), lambda i, j: (l, jf, 0, j)),
        pl.BlockSpec((1, 1, d, tf), lambda i, j: (l, jf, 0, nf + j)),
        pl.BlockSpec((1, 1, tf, d), lambda i, j: (l, jf, j, 0)),
    ]
    args = [x, mod4, g_norm, w_gu, w_gu, w_dn]
    if final:
        in_specs.append(pl.BlockSpec((1, d), lambda i, j: (0, 0)))
        args.append(g_final.reshape(1, d))
    return pl.pallas_call(
        functools.partial(_ffn_kernel, jm=jm, jn=jm, final=final),
        grid=(t // tm, nf),
        in_specs=in_specs,
        out_specs=pl.BlockSpec((tm, d), lambda i, j: (i, 0)),
        out_shape=jax.ShapeDtypeStruct((t, d), F32),
        scratch_shapes=[pltpu.VMEM((tm, d), BF16)],
        compiler_params=_params("parallel", "arbitrary"),
        name="ffn",
    )(*args)


def _rope_tables(n_pos, rot_dim):
    nf = rot_dim // 4
    pos = jnp.arange(n_pos)
    inv = ROPE_THETA ** (-jnp.arange(nf, dtype=F32) / nf)
    ang_r = (pos // GRID_W).astype(F32)[:, None] * inv
    ang_c = (pos % GRID_W).astype(F32)[:, None] * inv
    cos = jnp.concatenate([jnp.cos(ang_r), jnp.cos(ang_r), jnp.cos(ang_c), jnp.cos(ang_c)], axis=-1)
    sin = jnp.concatenate([-jnp.sin(ang_r), jnp.sin(ang_r), -jnp.sin(ang_c), jnp.sin(ang_c)], axis=-1)
    return cos, sin


def _rotate(y, cos, sin, nf):
    lanes = y.shape[-1]
    lane = lax.broadcasted_iota(jnp.int32, y.shape, 1)
    partner = jnp.where(lane % (2 * nf) < nf, pltpu.roll(y, lanes - nf, 1), pltpu.roll(y, nf, 1))
    return y * cos + partner * sin


def _proj_ab_kernel(*refs, rope):
    if rope:
        x_ref, mod_ref, g_ref, w_ref, wvt_ref, gq_ref, gk_ref, cos_ref, sin_ref, o_ref, vt_ref = refs
        cos, sin = cos_ref[...], sin_ref[...]
    else:
        x_ref, mod_ref, g_ref, w_ref, wvt_ref, gq_ref, gk_ref, o_ref, vt_ref = refs
    xn = _adaln(x_ref[...], g_ref, 1, mod_ref, 1).astype(BF16)
    vt_ref[...] = _nt_dot(wvt_ref[...], xn).astype(BF16)
    scale = HEAD_DIM ** -0.5 * LOG2E
    for c in range(N_COLS_AB // 2):
        y2 = jnp.dot(xn, w_ref[:, c * 256:(c + 1) * 256], preferred_element_type=F32)
        for hh in range(2):
            col = 2 * c + hh
            y = y2[:, hh * HEAD_DIM:(hh + 1) * HEAD_DIM]
            is_qa = col < COL_KA
            is_ka = COL_KA <= col < COL_VA
            is_qb = COL_QB <= col < COL_KB
            is_kb = COL_KB <= col < COL_VB
            if is_qa:
                y = _rms(y) * gq_ref[...]
            if is_ka:
                y = _rms(y) * gk_ref[...]
            if rope and (is_qa or is_ka or is_qb or is_kb):
                y = _rotate(y, cos, sin, HEAD_DIM // 4)
            if is_qa or is_qb:
                y = y * scale
            o_ref[:, col * HEAD_DIM:(col + 1) * HEAD_DIM] = y.astype(BF16)


def _proj_ab_call(x, mod4, g_norm, w_in, w_vt, g_q, g_k, tables, *, l, mod_base, rows_per_mod, n_pos):
    t, d = x.shape
    tm = min(256, t)
    rope = tables is not None
    const = dict(pipeline_mode=pl.Buffered(1))
    in_specs = [
        pl.BlockSpec((tm, d), lambda i: (i, 0)),
        pl.BlockSpec((1, 1, N_MOD, d), lambda i: (l, mod_base + (i * tm) // rows_per_mod, 0, 0)),
        pl.BlockSpec((1, 3, d), lambda i: (l, 0, 0)),
        pl.BlockSpec((d, AB_IN), lambda i: (0, 0), **const),
        pl.BlockSpec((KA + KB, d), lambda i: (0, 0), **const),
        pl.BlockSpec((1, HEAD_DIM), lambda i: (0, 0)),
        pl.BlockSpec((1, HEAD_DIM), lambda i: (0, 0)),
    ]
    args = [x, mod4, g_norm, w_in, w_vt, g_q.reshape(1, HEAD_DIM), g_k.reshape(1, HEAD_DIM)]
    if rope:
        npb = n_pos // tm
        in_specs += [pl.BlockSpec((tm, HEAD_DIM), lambda i: (i % npb, 0))] * 2
        args += list(tables)
    return pl.pallas_call(
        functools.partial(_proj_ab_kernel, rope=rope),
        grid=(t // tm,),
        in_specs=in_specs,
        out_specs=[pl.BlockSpec((tm, AB_IN), lambda i: (i, 0)), pl.BlockSpec((KA + KB, tm), lambda i: (0, i))],
        out_shape=[jax.ShapeDtypeStruct((t, AB_IN), BF16), jax.ShapeDtypeStruct((KA + KB, t), BF16)],
        compiler_params=_params("parallel"),
        name="proj_ab",
    )(*args)


def _proj_mla_kernel(*refs, rope):
    if rope:
        (x_ref, mod_ref, g_ref, wi_ref, gcq_ref, gckv_ref, wuq_ref, wuk_ref, wuvt_ref, cos_ref, sin_ref,
         q_ref, k_ref, vt_ref) = refs
        cos, sin = cos_ref[...], sin_ref[...]
    else:
        x_ref, mod_ref, g_ref, wi_ref, gcq_ref, gckv_ref, wuq_ref, wuk_ref, wuvt_ref, q_ref, k_ref, vt_ref = refs
    xn = _adaln(x_ref[...], g_ref, 1, mod_ref, 1).astype(BF16)
    t = jnp.dot(xn, wi_ref[...], preferred_element_type=F32)
    cq = (_rms(t[:, :MLA_Q_RANK]) * gcq_ref[...]).astype(BF16)
    ckv = (_rms(t[:, MLA_Q_RANK:MLA_Q_RANK + MLA_KV_RANK]) * gckv_ref[...]).astype(BF16)
    kr = t[:, MLA_Q_RANK + MLA_KV_RANK:]
    if rope:
        kr = _rotate(kr, cos, sin, MLA_ROPE // 4)
    kr = kr.astype(BF16)
    vt_ref[...] = _nt_dot(wuvt_ref[...], ckv).astype(BF16)
    scale = (MLA_NOPE + MLA_ROPE) ** -0.5 * LOG2E
    for h in range(MLA_HEADS):
        q = jnp.dot(cq, wuq_ref[:, h * MLA_QK:(h + 1) * MLA_QK], preferred_element_type=F32)
        qn, qr = q[:, :MLA_NOPE], q[:, MLA_NOPE:]
        if rope:
            qr = _rotate(qr, cos, sin, MLA_ROPE // 4)
        q_ref[:, h * MLA_QK:h * MLA_QK + MLA_NOPE] = (qn * scale).astype(BF16)
        q_ref[:, h * MLA_QK + MLA_NOPE:(h + 1) * MLA_QK] = (qr * scale).astype(BF16)
        k_ref[:, h * MLA_QK + MLA_NOPE:(h + 1) * MLA_QK] = kr
    for hp in range(MLA_HEADS // 2):
        kn = jnp.dot(ckv, wuk_ref[:, hp * 2 * MLA_NOPE:(hp + 1) * 2 * MLA_NOPE], preferred_element_type=F32)
        for hh in range(2):
            h = 2 * hp + hh
            k_ref[:, h * MLA_QK:h * MLA_QK + MLA_NOPE] = kn[:, hh * MLA_NOPE:(hh + 1) * MLA_NOPE].astype(BF16)


def _proj_mla_call(x, mod4, g_norm, w_in, g_cq, g_ckv, w_uq, w_uk, w_uvt, tables, *, l, mod_base, rows_per_mod,
                   n_pos):
    t, d = x.shape
    tm = min(256, t)
    rope = tables is not None
    const = dict(pipeline_mode=pl.Buffered(1))
    in_specs = [
        pl.BlockSpec((tm, d), lambda i: (i, 0)),
        pl.BlockSpec((1, 1, N_MOD, d), lambda i: (l, mod_base + (i * tm) // rows_per_mod, 0, 0)),
        pl.BlockSpec((1, 3, d), lambda i: (l, 0, 0)),
        pl.BlockSpec((d, C_IN_PAD), lambda i: (0, 0), **const),
        pl.BlockSpec((1, MLA_Q_RANK), lambda i: (0, 0)),
        pl.BlockSpec((1, MLA_KV_RANK), lambda i: (0, 0)),
        pl.BlockSpec((MLA_Q_RANK, MLA_HEADS * MLA_QK), lambda i: (0, 0), **const),
        pl.BlockSpec((MLA_KV_RANK, MLA_HEADS * MLA_NOPE), lambda i: (0, 0), **const),
        pl.BlockSpec((MLA_HEADS * MLA_V, MLA_KV_RANK), lambda i: (0, 0), **const),
    ]
    args = [x, mod4, g_norm, w_in, g_cq.reshape(1, -1), g_ckv.reshape(1, -1), w_uq, w_uk, w_uvt]
    if rope:
        npb = n_pos // tm
        in_specs += [pl.BlockSpec((tm, 128), lambda i: (i % npb, 0))] * 2
        args += list(tables)
    return pl.pallas_call(
        functools.partial(_proj_mla_kernel, rope=rope),
        grid=(t // tm,),
        in_specs=in_specs,
        out_specs=[
            pl.BlockSpec((tm, MLA_HEADS * MLA_QK), lambda i: (i, 0)),
            pl.BlockSpec((tm, MLA_HEADS * MLA_QK), lambda i: (i, 0)),
            pl.BlockSpec((MLA_HEADS * MLA_V, tm), lambda i: (0, i)),
        ],
        out_shape=[
            jax.ShapeDtypeStruct((t, MLA_HEADS * MLA_QK), BF16),
            jax.ShapeDtypeStruct((t, MLA_HEADS * MLA_QK), BF16),
            jax.ShapeDtypeStruct((MLA_HEADS * MLA_V, t), BF16),
        ],
        compiler_params=_params("parallel"),
        name="proj_mla",
    )(*args)


def _nt_dot(a, b):
    return lax.dot_general(a, b, (((1,), (1,)), ((), ())), preferred_element_type=F32)


def _flash_kernel(*refs, group, dq, dv, tk, n_lat, has_sink, unroll):
    refs = list(refs)
    q_ref, kc_ref, vc_ref = refs[:3]
    pos = 3
    if n_lat:
        kl_ref, vl_ref = refs[pos:pos + 2]
        pos += 2
    if has_sink:
        sink_ref = refs[pos]
        pos += 1
    o_ref = refs[pos]
    if n_lat:
        s_ref = refs[pos + 1]
    tq = q_ref.shape[0]
    q = jnp.concatenate([q_ref[:, g * dq:(g + 1) * dq] for g in range(group)], axis=0) if group > 1 else q_ref[...]

    s = _nt_dot(kc_ref[...], q)
    m = jnp.max(s, axis=0, keepdims=True)
    if has_sink:
        head0 = pl.program_id(1) * group
        sink = jnp.concatenate(
            [jnp.full((1, tq), sink_ref[head0 + g] * LOG2E, F32) for g in range(group)], axis=1)
        m = jnp.maximum(m, sink)
    p = jnp.exp2(s - m)
    l = jnp.sum(p, axis=0, keepdims=True)
    if has_sink:
        l = l + jnp.exp2(sink - m)
    acc = jnp.dot(vc_ref[...], p.astype(BF16), preferred_element_type=F32)

    if n_lat:
        def scores(t, slot):
            s_ref[slot] = _nt_dot(kl_ref[pl.ds(pl.multiple_of(t * tk, tk), tk), :], q)

        def update(t, slot, m, l, acc):
            s = s_ref[slot]
            m_new = jnp.maximum(m, jnp.max(s, axis=0, keepdims=True))
            alpha = jnp.exp2(m - m_new)
            p = jnp.exp2(s - m_new)
            l = alpha * l + jnp.sum(p, axis=0, keepdims=True)
            r0 = pl.multiple_of(t * tk, tk)
            acc = alpha * acc + jnp.dot(vl_ref[:, pl.ds(r0, tk)], p.astype(BF16), preferred_element_type=F32)
            return m_new, l, acc

        def pair(tt, carry):
            t0 = 2 * tt
            scores(t0 + 1, 1)
            carry = update(t0, 0, *carry)
            scores(jnp.minimum(t0 + 2, n_lat - 1), 0)
            return update(t0 + 1, 1, *carry)

        scores(0, 0)
        m, l, acc = lax.fori_loop(0, n_lat // 2, pair, (m, l, acc), unroll=unroll)
        if n_lat % 2:
            m, l, acc = update(n_lat - 1, 0, m, l, acc)

    o = (acc / l).T.astype(BF16)
    for g in range(group):
        o_ref[:, g * dv:(g + 1) * dv] = o[g * tq:(g + 1) * tq]


def _flash_call(q_arr, kv_ctx, kv_lat, sink, *, batch, n_q, n_ctx, n_heads, group, dq, dv, q_col, k_col, v_col, tq):
    n_kv = n_heads // group
    nqb = n_q // tq
    has_sink = sink is not None
    in_specs = [
        pl.BlockSpec((tq, group * dq), lambda b, g, i: (b * nqb + i, q_col // group + g)),
        pl.BlockSpec((n_ctx, dq), lambda b, g, i: (b, k_col + g)),
        pl.BlockSpec((dv, n_ctx), lambda b, g, i: (v_col + g, b)),
    ]
    args = [q_arr, kv_ctx[0], kv_ctx[1]]
    n_lat = 0
    tk = 0
    if kv_lat is not None:
        n_rows = kv_lat[0].shape[0] // batch
        tk = min(512, n_rows)
        n_lat = n_rows // tk
        in_specs += [
            pl.BlockSpec((n_rows, dq), lambda b, g, i: (b, k_col + g)),
            pl.BlockSpec((dv, n_rows), lambda b, g, i: (v_col + g, b)),
        ]
        args += [kv_lat[0], kv_lat[1]]
    if has_sink:
        in_specs.append(pl.BlockSpec(memory_space=pltpu.SMEM))
        args.append(sink)
    return pl.pallas_call(
        functools.partial(_flash_kernel, group=group, dq=dq, dv=dv, tk=tk, n_lat=n_lat, has_sink=has_sink,
                          unroll=max(1, min(FLASH_UNROLL, n_lat // 2))),
        grid=(batch, n_kv, nqb),
        scratch_shapes=[pltpu.VMEM((2, tk, group * tq), F32)] if n_lat else [],
        in_specs=in_specs,
        out_specs=pl.BlockSpec((tq, group * dv), lambda b, g, i: (b * nqb + i, g)),
        out_shape=jax.ShapeDtypeStruct((batch * n_q, n_heads * dv), BF16),
        compiler_params=_params("parallel", "parallel", "arbitrary"),
        name="flash",
    )(*args)


def _window_kernel(q_ref, kp_ref, kc_ref, kn_ref, vp_ref, vc_ref, vn_ref, kx_ref, vx_ref, sink_ref, o_ref):
    i = pl.program_id(2)
    nblk = pl.num_programs(2)
    qb = Q_BLOCK
    q = jnp.concatenate([q_ref[:, g * HEAD_DIM:(g + 1) * HEAD_DIM] for g in range(GROUP_B)], axis=0)
    rows = GROUP_B * qb
    r = lax.broadcasted_iota(jnp.int32, (rows, qb), 0) % qb
    c = lax.broadcasted_iota(jnp.int32, (rows, qb), 1)
    off_p = jnp.where(i > 0, 0, qb)
    off_n = jnp.where(i < nblk - 1, 0, qb)
    s_p = jnp.where(c >= r + off_p, _nt_dot(q, kp_ref[...]), NEG_INF)
    s_c = _nt_dot(q, kc_ref[...])
    s_n = jnp.where(c + off_n <= r, _nt_dot(q, kn_ref[...]), NEG_INF)
    s_x = _nt_dot(q, kx_ref[...])
    head0 = pl.program_id(1) * GROUP_B
    sink = jnp.concatenate([jnp.full((qb, 1), sink_ref[head0 + g] * LOG2E, F32) for g in range(GROUP_B)], axis=0)
    m = jnp.maximum(jnp.maximum(jnp.max(s_p, -1, keepdims=True), jnp.max(s_c, -1, keepdims=True)),
                    jnp.maximum(jnp.max(s_n, -1, keepdims=True), jnp.max(s_x, -1, keepdims=True)))
    m = jnp.maximum(m, sink)
    p_p, p_c, p_n, p_x = jnp.exp2(s_p - m), jnp.exp2(s_c - m), jnp.exp2(s_n - m), jnp.exp2(s_x - m)
    l = (jnp.sum(p_p, -1, keepdims=True) + jnp.sum(p_c, -1, keepdims=True)
         + jnp.sum(p_n, -1, keepdims=True) + jnp.sum(p_x, -1, keepdims=True) + jnp.exp2(sink - m))
    acc = (jnp.dot(p_p.astype(BF16), vp_ref[...], preferred_element_type=F32)
           + jnp.dot(p_c.astype(BF16), vc_ref[...], preferred_element_type=F32)
           + jnp.dot(p_n.astype(BF16), vn_ref[...], preferred_element_type=F32)
           + jnp.dot(p_x.astype(BF16), vx_ref[...], preferred_element_type=F32))
    o = (acc / l).astype(BF16)
    for g in range(GROUP_B):
        o_ref[:, g * HEAD_DIM:(g + 1) * HEAD_DIM] = o[g * qb:(g + 1) * qb]


def _window_call(qkv_x, qkv_c, sink, *, batch, n, n_ctx):
    assert WINDOW == Q_BLOCK
    nblk = n // Q_BLOCK
    qb = Q_BLOCK

    def band(col, shift):
        def index(b, g, i):
            return (b * nblk + jnp.clip(i + shift, 0, nblk - 1), col + g)
        return pl.BlockSpec((qb, HEAD_DIM), index)

    in_specs = [
        pl.BlockSpec((qb, GROUP_B * HEAD_DIM), lambda b, g, i: (b * nblk + i, COL_QB // GROUP_B + g)),
        band(COL_KB, -1), band(COL_KB, 0), band(COL_KB, 1),
        band(COL_VB, -1), band(COL_VB, 0), band(COL_VB, 1),
        pl.BlockSpec((n_ctx, HEAD_DIM), lambda b, g, i: (b, COL_KB + g)),
        pl.BlockSpec((n_ctx, HEAD_DIM), lambda b, g, i: (b, COL_VB + g)),
        pl.BlockSpec(memory_space=pltpu.SMEM),
    ]
    return pl.pallas_call(
        _window_kernel,
        grid=(batch, N_KV_B, nblk),
        in_specs=in_specs,
        out_specs=pl.BlockSpec((qb, GROUP_B * HEAD_DIM), lambda b, g, i: (b * nblk + i, g)),
        out_shape=jax.ShapeDtypeStruct((batch * n, QB), BF16),
        compiler_params=_params("parallel", "parallel", "arbitrary"),
        name="window",
    )(qkv_x, qkv_x, qkv_x, qkv_x, qkv_x, qkv_x, qkv_x, qkv_c, qkv_c, sink)


def _out_kernel(*refs, n_in):
    x_ref, mod_ref = refs[:2]
    o_ref = refs[2 + 2 * n_in]
    y = None
    for k in range(n_in):
        part = jnp.dot(refs[2 + 2 * k][...], refs[3 + 2 * k][...], preferred_element_type=F32)
        y = part if y is None else y + part
    o_ref[...] = x_ref[...] + _mod_row(mod_ref, 5) * y


def _out_call(x, mod4, pairs, *, l, mod_base, rows_per_mod):
    t, d = x.shape
    tm = min(512, t)
    in_specs = [
        pl.BlockSpec((tm, d), lambda i: (i, 0)),
        pl.BlockSpec((1, 1, N_MOD, d), lambda i: (l, mod_base + (i * tm) // rows_per_mod, 0, 0)),
    ]
    args = [x, mod4]
    for o, w in pairs:
        kk = o.shape[1]
        in_specs += [
            pl.BlockSpec((tm, kk), lambda i: (i, 0)),
            pl.BlockSpec((kk, d), lambda i: (0, 0), pipeline_mode=pl.Buffered(1)),
        ]
        args += [o, w]
    return pl.pallas_call(
        functools.partial(_out_kernel, n_in=len(pairs)),
        grid=(t // tm,),
        in_specs=in_specs,
        out_specs=pl.BlockSpec((tm, d), lambda i: (i, 0)),
        out_shape=jax.ShapeDtypeStruct((t, d), F32),
        compiler_params=_params("parallel"),
        name="out_proj",
    )(*args)


def _mla_weight_layout(w_in_c, w_uq, w_ukv):
    d = w_in_c.shape[0]
    w_in = jnp.concatenate([w_in_c, jnp.zeros((d, C_IN_PAD - w_in_c.shape[1]), w_in_c.dtype)], axis=1)
    wq = w_uq.reshape(MLA_Q_RANK, MLA_HEADS, MLA_NOPE + MLA_ROPE)
    wq = jnp.concatenate([wq, jnp.zeros((MLA_Q_RANK, MLA_HEADS, MLA_QK - MLA_NOPE - MLA_ROPE), w_uq.dtype)], axis=-1)
    wkv = w_ukv.reshape(MLA_KV_RANK, MLA_HEADS, MLA_NOPE + MLA_V)
    wk = wkv[:, :, :MLA_NOPE].reshape(MLA_KV_RANK, MLA_HEADS * MLA_NOPE)
    wvt = wkv[:, :, MLA_NOPE:].reshape(MLA_KV_RANK, MLA_HEADS * MLA_V).T
    return (w_in.astype(BF16), wq.reshape(MLA_Q_RANK, MLA_HEADS * MLA_QK).astype(BF16), wk.astype(BF16),
            wvt.astype(BF16))


def kernel(x, c, ctx, c_ctx, w_mod, b_mod, g_norm, w_gate_up, w_down, w_in_ab, g_qnorm_a, g_knorm_a,
           sink_b, w_out_ab, w_in_c, g_cq, g_ckv, w_uq, w_ukv, w_out_c, g_final):
    batch, n, d = x.shape
    n_ctx = ctx.shape[1]
    depth = w_mod.shape[0]
    assert batch + 1 <= 8 and n % GRID_W == 0

    cc = jnp.concatenate([c, c_ctx[None], jnp.zeros((8 - batch - 1, d), F32)], axis=0)
    mod4 = _mod_call(cc, w_mod, b_mod)

    w_gu = w_gate_up.astype(BF16)
    w_dn = w_down.astype(BF16)
    w_ab = w_in_ab.astype(BF16)
    w_oab = w_out_ab.astype(BF16)
    w_oc = w_out_c.astype(BF16)

    cos_h, sin_h = _rope_tables(n, HEAD_DIM)
    cos_m, sin_m = _rope_tables(n, MLA_ROPE)
    pad = jnp.zeros((n, 128 - MLA_ROPE), F32)
    tables_h = (cos_h, sin_h)
    tables_m = (jnp.concatenate([cos_m, pad], axis=1), jnp.concatenate([sin_m, pad], axis=1))

    xs = x.reshape(batch * n, d)
    hs = ctx.reshape(batch * n_ctx, d)
    x_mod = dict(mod_base=0, rows_per_mod=n)
    c_mod = dict(mod_base=batch, rows_per_mod=batch * n_ctx)

    for l in range(depth):
        need_ctx = l < depth - 1
        last = l == depth - 1
        i = l // 2
        xs = _ffn_call(xs, mod4, g_norm, w_gu, w_dn, l=l, jf=0, jm=0, **x_mod)
        hs = _ffn_call(hs, mod4, g_norm, w_gu, w_dn, l=l, jf=0, jm=0, **c_mod)
        if l % 2 == 0:
            w_vt = jnp.concatenate([w_ab[i][:, COL_VA * HEAD_DIM:COL_QB * HEAD_DIM],
                                    w_ab[i][:, COL_VB * HEAD_DIM:]], axis=1).T
            qkv_x, vt_x = _proj_ab_call(xs, mod4, g_norm, w_ab[i], w_vt, g_qnorm_a[i], g_knorm_a[i], tables_h,
                                        l=l, n_pos=n, **x_mod)
            qkv_c, vt_c = _proj_ab_call(hs, mod4, g_norm, w_ab[i], w_vt, g_qnorm_a[i], g_knorm_a[i], None,
                                        l=l, n_pos=n, **c_mod)
            oa = _flash_call(qkv_x, (qkv_c, vt_c), (qkv_x, vt_x), None, batch=batch, n_q=n, n_ctx=n_ctx,
                             n_heads=N_HEADS_A, group=GROUP_A, dq=HEAD_DIM, dv=HEAD_DIM,
                             q_col=COL_QA, k_col=COL_KA, v_col=0, tq=min(128, n))
            ob = _window_call(qkv_x, qkv_c, sink_b[i], batch=batch, n=n, n_ctx=n_ctx)
            w_o = w_oab[i]
            xs_new = _out_call(xs, mod4, [(oa, w_o[:QA]), (ob, w_o[QA:])], l=l, **x_mod)
            if need_ctx:
                oa_c = _flash_call(qkv_c, (qkv_c, vt_c), None, None, batch=batch, n_q=n_ctx, n_ctx=n_ctx,
                                   n_heads=N_HEADS_A, group=GROUP_A, dq=HEAD_DIM, dv=HEAD_DIM,
                                   q_col=COL_QA, k_col=COL_KA, v_col=0, tq=min(128, n_ctx))
                ob_c = _flash_call(qkv_c, (qkv_c, vt_c), None, sink_b[i], batch=batch, n_q=n_ctx, n_ctx=n_ctx,
                                   n_heads=N_HEADS_B, group=GROUP_B, dq=HEAD_DIM, dv=HEAD_DIM,
                                   q_col=COL_QB, k_col=COL_KB, v_col=N_KV_A, tq=min(128, n_ctx))
                hs = _out_call(hs, mod4, [(oa_c, w_o[:QA]), (ob_c, w_o[QA:])], l=l, **c_mod)
            xs = xs_new
        else:
            w_in, wq, wk, wvt = _mla_weight_layout(w_in_c[i], w_uq[i], w_ukv[i])
            q_x, k_x, v_x = _proj_mla_call(xs, mod4, g_norm, w_in, g_cq[i], g_ckv[i], wq, wk, wvt, tables_m,
                                           l=l, n_pos=n, **x_mod)
            q_c, k_c, v_c = _proj_mla_call(hs, mod4, g_norm, w_in, g_cq[i], g_ckv[i], wq, wk, wvt, None,
                                           l=l, n_pos=n, **c_mod)
            o = _flash_call(q_x, (k_c, v_c), (k_x, v_x), None, batch=batch, n_q=n, n_ctx=n_ctx,
                            n_heads=MLA_HEADS, group=1, dq=MLA_QK, dv=MLA_V, q_col=0, k_col=0, v_col=0,
                            tq=min(512, n))
            xs_new = _out_call(xs, mod4, [(o, w_oc[i])], l=l, **x_mod)
            if need_ctx:
                o_c = _flash_call(q_c, (k_c, v_c), None, None, batch=batch, n_q=n_ctx, n_ctx=n_ctx,
                                  n_heads=MLA_HEADS, group=1, dq=MLA_QK, dv=MLA_V, q_col=0, k_col=0, v_col=0,
                                  tq=n_ctx)
                hs = _out_call(hs, mod4, [(o_c, w_oc[i])], l=l, **c_mod)
            xs = xs_new
        xs = _ffn_call(xs, mod4, g_norm, w_gu, w_dn, l=l, jf=1, jm=2, g_final=g_final if last else None, **x_mod)
        if need_ctx:
            hs = _ffn_call(hs, mod4, g_norm, w_gu, w_dn, l=l, jf=1, jm=2, **c_mod)
    return xs.reshape(batch, n, d)
```

```python
import functools

import jax
import jax.numpy as jnp
from jax import lax
from jax.experimental import pallas as pl
from jax.experimental.pallas import tpu as pltpu

GRID_W = 64
HEAD_DIM = 128
N_HEADS_A = 8
N_KV_A = 2
N_HEADS_B = 8
N_KV_B = 2
WINDOW = 128
Q_BLOCK = 128
ROPE_THETA = 10000.0
MLA_HEADS = 16
MLA_Q_RANK = 512
MLA_KV_RANK = 256
MLA_NOPE = 128
MLA_ROPE = 64
MLA_V = 128
MLA_QK = 256
N_MOD = 9
EPS = 1e-6
NEG_INF = -1e30
LOG2E = 1.4426950408889634

QA = N_HEADS_A * HEAD_DIM
KA = N_KV_A * HEAD_DIM
QB = N_HEADS_B * HEAD_DIM
KB = N_KV_B * HEAD_DIM
AB_IN = QA + 2 * KA + QB + 2 * KB
GROUP_A = N_HEADS_A // N_KV_A
GROUP_B = N_HEADS_B // N_KV_B
COL_QA = 0
COL_KA = N_HEADS_A
COL_VA = COL_KA + N_KV_A
COL_QB = COL_VA + N_KV_A
COL_KB = COL_QB + N_HEADS_B
COL_VB = COL_KB + N_KV_B
N_COLS_AB = COL_VB + N_KV_B
C_IN_PAD = MLA_Q_RANK + MLA_KV_RANK + 128

F32 = jnp.float32
BF16 = jnp.bfloat16
V7X_VMEM_LIMIT_BYTES = 56 * 1024 * 1024
FLASH_UNROLL = 4
WINDOW_QBLOCKS = 2
ADALN_ROWS = 16
ADALN_UNROLL = 8


def _params(*sem):
    return pltpu.CompilerParams(dimension_semantics=sem, vmem_limit_bytes=V7X_VMEM_LIMIT_BYTES)


def _rms(x):
    return x * lax.rsqrt(jnp.mean(x * x, axis=-1, keepdims=True) + EPS)


def _silu(x):
    return x / (1.0 + jnp.exp(-x))


def _mod_row(mod_ref, k):
    return mod_ref[0, 0, k:k + 1, :]


def _adaln(x, g_ref, jn, mod_ref, j):
    return _rms(x) * g_ref[0, jn:jn + 1, :] * (1.0 + _mod_row(mod_ref, 3 * j + 1)) + _mod_row(mod_ref, 3 * j)


def _adaln_rows(x_ref, out_ref, g_ref, jn, mod_ref, j):
    rows = ADALN_ROWS
    gain = g_ref[0, jn:jn + 1, :] * (1.0 + _mod_row(mod_ref, 3 * j + 1))
    shift = _mod_row(mod_ref, 3 * j)

    def body(r, carry):
        r0 = pl.multiple_of(r * rows, rows)
        x = x_ref[pl.ds(r0, rows), :]
        out_ref[pl.ds(r0, rows), :] = (_rms(x) * gain + shift).astype(BF16)
        return carry

    n_groups = x_ref.shape[0] // rows
    lax.fori_loop(0, n_groups, body, 0, unroll=min(ADALN_UNROLL, n_groups))


def _mod_kernel(c_ref, w_ref, b_ref, o_ref):
    sc = _silu(c_ref[...]).astype(BF16)
    o_ref[0] = jnp.dot(sc, w_ref[0].astype(BF16), preferred_element_type=F32) + b_ref[0]


def _mod_call(cc, w_mod, b_mod):
    depth, d, nd = w_mod.shape
    tn = min(d, 1024)
    out = pl.pallas_call(
        _mod_kernel,
        grid=(depth, nd // tn),
        in_specs=[
            pl.BlockSpec((8, d), lambda l, j: (0, 0)),
            pl.BlockSpec((1, d, tn), lambda l, j: (l, 0, j)),
            pl.BlockSpec((1, 1, tn), lambda l, j: (l, 0, j)),
        ],
        out_specs=pl.BlockSpec((1, 8, tn), lambda l, j: (l, 0, j)),
        out_shape=jax.ShapeDtypeStruct((depth, 8, nd), F32),
        compiler_params=_params("parallel", "parallel"),
        name="mod",
    )(cc, w_mod, b_mod.reshape(depth, 1, nd))
    return out.reshape(depth, 8, N_MOD, d)


def _ffn_kernel(*refs, jm, jn, final):
    if final:
        x_ref, mod_ref, g_ref, wg_ref, wu_ref, wd_ref, gf_ref, o_ref, xn_ref = refs
    else:
        x_ref, mod_ref, g_ref, wg_ref, wu_ref, wd_ref, o_ref, xn_ref = refs
    j = pl.program_id(1)

    @pl.when(j == 0)
    def _():
        _adaln_rows(x_ref, xn_ref, g_ref, jn, mod_ref, jm)
        o_ref[...] = jnp.zeros_like(o_ref)

    xn = xn_ref[...]
    gt = jnp.dot(xn, wg_ref[0, 0], preferred_element_type=F32)
    up = jnp.dot(xn, wu_ref[0, 0], preferred_element_type=F32)
    h = (_silu(gt) * up).astype(BF16)
    o_ref[...] += jnp.dot(h, wd_ref[0, 0], preferred_element_type=F32)

    @pl.when(j == pl.num_programs(1) - 1)
    def _():
        y = x_ref[...] + (0.5 * _mod_row(mod_ref, 3 * jm + 2)) * o_ref[...]
        if final:
            y = _rms(y) * gf_ref[...]
        o_ref[...] = y


def _ffn_call(x, mod4, g_norm, w_gu, w_dn, *, l, jf, jm, mod_base, rows_per_mod, g_final=None):
    t, d = x.shape
    f = w_dn.shape[2]
    tm = min(512, t)
    tf = min(512, f)
    nf = f // tf
    final = g_final is not None
    in_specs = [
        pl.BlockSpec((tm, d), lambda i, j: (i, 0)),
        pl.BlockSpec((1, 1, N_MOD, d), lambda i, j: (l, mod_base + (i * tm) // rows_per_mod, 0, 0)),
        pl.BlockSpec((1, 3, d), lambda i, j: (l, 0, 0)),
        pl.BlockSpec((1, 1, d, tf), lambda i, j: (l, jf, 0, j)),
        pl.BlockSpec((1, 1, d, tf), lambda i, j: (l, jf, 0, nf + j)),
        pl.BlockSpec((1, 1, tf, d), lambda i, j: (l, jf, j, 0)),
    ]
    args = [x, mod4, g_norm, w_gu, w_gu, w_dn]
    if final:
        in_specs.append(pl.BlockSpec((1, d), lambda i, j: (0, 0)))
        args.append(g_final.reshape(1, d))
    return pl.pallas_call(
        functools.partial(_ffn_kernel, jm=jm, jn=jm, final=final),
        grid=(t // tm, nf),
        in_specs=in_specs,
        out_specs=pl.BlockSpec((tm, d), lambda i, j: (i, 0)),
        out_shape=jax.ShapeDtypeStruct((t, d), F32),
        scratch_shapes=[pltpu.VMEM((tm, d), BF16)],
        compiler_params=_params("parallel", "arbitrary"),
        name="ffn",
    )(*args)


def _rope_tables(n_pos, rot_dim):
    nf = rot_dim // 4
    pos = jnp.arange(n_pos)
    inv = ROPE_THETA ** (-jnp.arange(nf, dtype=F32) / nf)
    ang_r = (pos // GRID_W).astype(F32)[:, None] * inv
    ang_c = (pos % GRID_W).astype(F32)[:, None] * inv
    cos = jnp.concatenate([jnp.cos(ang_r), jnp.cos(ang_r), jnp.cos(ang_c), jnp.cos(ang_c)], axis=-1)
    sin = jnp.concatenate([-jnp.sin(ang_r), jnp.sin(ang_r), -jnp.sin(ang_c), jnp.sin(ang_c)], axis=-1)
    return cos, sin


def _rotate(y, cos, sin, nf):
    lanes = y.shape[-1]
    lane = lax.broadcasted_iota(jnp.int32, y.shape, 1)
    partner = jnp.where(lane % (2 * nf) < nf, pltpu.roll(y, lanes - nf, 1), pltpu.roll(y, nf, 1))
    return y * cos + partner * sin


def _proj_ab_kernel(*refs, rope):
    if rope:
        x_ref, mod_ref, g_ref, w_ref, wvt_ref, gq_ref, gk_ref, cos_ref, sin_ref, o_ref, vt_ref = refs
        cos, sin = cos_ref[...], sin_ref[...]
    else:
        x_ref, mod_ref, g_ref, w_ref, wvt_ref, gq_ref, gk_ref, o_ref, vt_ref = refs
    xn = _adaln(x_ref[...], g_ref, 1, mod_ref, 1).astype(BF16)
    vt_ref[...] = _nt_dot(wvt_ref[...], xn).astype(BF16)
    scale = HEAD_DIM ** -0.5 * LOG2E
    for c in range(N_COLS_AB // 2):
        y2 = jnp.dot(xn, w_ref[:, c * 256:(c + 1) * 256], preferred_element_type=F32)
        for hh in range(2):
            col = 2 * c + hh
            y = y2[:, hh * HEAD_DIM:(hh + 1) * HEAD_DIM]
            is_qa = col < COL_KA
            is_ka = COL_KA <= col < COL_VA
            is_qb = COL_QB <= col < COL_KB
            is_kb = COL_KB <= col < COL_VB
            if is_qa:
                y = _rms(y) * gq_ref[...]
            if is_ka:
                y = _rms(y) * gk_ref[...]
            if rope and (is_qa or is_ka or is_qb or is_kb):
                y = _rotate(y, cos, sin, HEAD_DIM // 4)
            if is_qa or is_qb:
                y = y * scale
            o_ref[:, col * HEAD_DIM:(col + 1) * HEAD_DIM] = y.astype(BF16)


def _proj_ab_call(x, mod4, g_norm, w_in, w_vt, g_q, g_k, tables, *, l, mod_base, rows_per_mod, n_pos):
    t, d = x.shape
    tm = min(256, t)
    rope = tables is not None
    const = dict(pipeline_mode=pl.Buffered(1))
    in_specs = [
        pl.BlockSpec((tm, d), lambda i: (i, 0)),
        pl.BlockSpec((1, 1, N_MOD, d), lambda i: (l, mod_base + (i * tm) // rows_per_mod, 0, 0)),
        pl.BlockSpec((1, 3, d), lambda i: (l, 0, 0)),
        pl.BlockSpec((d, AB_IN), lambda i: (0, 0), **const),
        pl.BlockSpec((KA + KB, d), lambda i: (0, 0), **const),
        pl.BlockSpec((1, HEAD_DIM), lambda i: (0, 0)),
        pl.BlockSpec((1, HEAD_DIM), lambda i: (0, 0)),
    ]
    args = [x, mod4, g_norm, w_in, w_vt, g_q.reshape(1, HEAD_DIM), g_k.reshape(1, HEAD_DIM)]
    if rope:
        npb = n_pos // tm
        in_specs += [pl.BlockSpec((tm, HEAD_DIM), lambda i: (i % npb, 0))] * 2
        args += list(tables)
    return pl.pallas_call(
        functools.partial(_proj_ab_kernel, rope=rope),
        grid=(t // tm,),
        in_specs=in_specs,
        out_specs=[pl.BlockSpec((tm, AB_IN), lambda i: (i, 0)), pl.BlockSpec((KA + KB, tm), lambda i: (0, i))],
        out_shape=[jax.ShapeDtypeStruct((t, AB_IN), BF16), jax.ShapeDtypeStruct((KA + KB, t), BF16)],
        compiler_params=_params("parallel"),
        name="proj_ab",
    )(*args)


def _proj_mla_kernel(*refs, rope):
    if rope:
        (x_ref, mod_ref, g_ref, wi_ref, gcq_ref, gckv_ref, wuq_ref, wuk_ref, wuvt_ref, cos_ref, sin_ref,
         q_ref, k_ref, vt_ref) = refs
        cos, sin = cos_ref[...], sin_ref[...]
    else:
        x_ref, mod_ref, g_ref, wi_ref, gcq_ref, gckv_ref, wuq_ref, wuk_ref, wuvt_ref, q_ref, k_ref, vt_ref = refs
    xn = _adaln(x_ref[...], g_ref, 1, mod_ref, 1).astype(BF16)
    t = jnp.dot(xn, wi_ref[...], preferred_element_type=F32)
    cq = (_rms(t[:, :MLA_Q_RANK]) * gcq_ref[...]).astype(BF16)
    ckv = (_rms(t[:, MLA_Q_RANK:MLA_Q_RANK + MLA_KV_RANK]) * gckv_ref[...]).astype(BF16)
    kr = t[:, MLA_Q_RANK + MLA_KV_RANK:]
    if rope:
        kr = _rotate(kr, cos, sin, MLA_ROPE // 4)
    kr = kr.astype(BF16)
    vt_ref[...] = _nt_dot(wuvt_ref[...], ckv).astype(BF16)
    scale = (MLA_NOPE + MLA_ROPE) ** -0.5 * LOG2E
    for h in range(MLA_HEADS):
        q = jnp.dot(cq, wuq_ref[:, h * MLA_QK:(h + 1) * MLA_QK], preferred_element_type=F32)
        qn, qr = q[:, :MLA_NOPE], q[:, MLA_NOPE:]
        if rope:
            qr = _rotate(qr, cos, sin, MLA_ROPE // 4)
        q_ref[:, h * MLA_QK:h * MLA_QK + MLA_NOPE] = (qn * scale).astype(BF16)
        q_ref[:, h * MLA_QK + MLA_NOPE:(h + 1) * MLA_QK] = (qr * scale).astype(BF16)
        k_ref[:, h * MLA_QK + MLA_NOPE:(h + 1) * MLA_QK] = kr
    for hp in range(MLA_HEADS // 2):
        kn = jnp.dot(ckv, wuk_ref[:, hp * 2 * MLA_NOPE:(hp + 1) * 2 * MLA_NOPE], preferred_element_type=F32)
        for hh in range(2):
            h = 2 * hp + hh
            k_ref[:, h * MLA_QK:h * MLA_QK + MLA_NOPE] = kn[:, hh * MLA_NOPE:(hh + 1) * MLA_NOPE].astype(BF16)


def _proj_mla_call(x, mod4, g_norm, w_in, g_cq, g_ckv, w_uq, w_uk, w_uvt, tables, *, l, mod_base, rows_per_mod,
                   n_pos):
    t, d = x.shape
    tm = min(256, t)
    rope = tables is not None
    const = dict(pipeline_mode=pl.Buffered(1))
    in_specs = [
        pl.BlockSpec((tm, d), lambda i: (i, 0)),
        pl.BlockSpec((1, 1, N_MOD, d), lambda i: (l, mod_base + (i * tm) // rows_per_mod, 0, 0)),
        pl.BlockSpec((1, 3, d), lambda i: (l, 0, 0)),
        pl.BlockSpec((d, C_IN_PAD), lambda i: (0, 0), **const),
        pl.BlockSpec((1, MLA_Q_RANK), lambda i: (0, 0)),
        pl.BlockSpec((1, MLA_KV_RANK), lambda i: (0, 0)),
        pl.BlockSpec((MLA_Q_RANK, MLA_HEADS * MLA_QK), lambda i: (0, 0), **const),
        pl.BlockSpec((MLA_KV_RANK, MLA_HEADS * MLA_NOPE), lambda i: (0, 0), **const),
        pl.BlockSpec((MLA_HEADS * MLA_V, MLA_KV_RANK), lambda i: (0, 0), **const),
    ]
    args = [x, mod4, g_norm, w_in, g_cq.reshape(1, -1), g_ckv.reshape(1, -1), w_uq, w_uk, w_uvt]
    if rope:
        npb = n_pos // tm
        in_specs += [pl.BlockSpec((tm, 128), lambda i: (i % npb, 0))] * 2
        args += list(tables)
    return pl.pallas_call(
        functools.partial(_proj_mla_kernel, rope=rope),
        grid=(t // tm,),
        in_specs=in_specs,
        out_specs=[
            pl.BlockSpec((tm, MLA_HEADS * MLA_QK), lambda i: (i, 0)),
            pl.BlockSpec((tm, MLA_HEADS * MLA_QK), lambda i: (i, 0)),
            pl.BlockSpec((MLA_HEADS * MLA_V, tm), lambda i: (0, i)),
        ],
        out_shape=[
            jax.ShapeDtypeStruct((t, MLA_HEADS * MLA_QK), BF16),
            jax.ShapeDtypeStruct((t, MLA_HEADS * MLA_QK), BF16),
            jax.ShapeDtypeStruct((MLA_HEADS * MLA_V, t), BF16),
        ],
        compiler_params=_params("parallel"),
        name="proj_mla",
    )(*args)


def _nt_dot(a, b):
    return lax.dot_general(a, b, (((1,), (1,)), ((), ())), preferred_element_type=F32)


def _flash_kernel(*refs, group, dq, dv, tk, n_lat, has_sink, unroll):
    refs = list(refs)
    q_ref, kc_ref, vc_ref = refs[:3]
    pos = 3
    if n_lat:
        kl_ref, vl_ref = refs[pos:pos + 2]
        pos += 2
    if has_sink:
        sink_ref = refs[pos]
        pos += 1
    o_ref = refs[pos]
    if n_lat:
        s_ref = refs[pos + 1]
    tq = q_ref.shape[0]
    q = jnp.concatenate([q_ref[:, g * dq:(g + 1) * dq] for g in range(group)], axis=0) if group > 1 else q_ref[...]

    def scores(t, slot):
        s_ref[slot] = _nt_dot(kl_ref[pl.ds(pl.multiple_of(t * tk, tk), tk), :], q)

    s = _nt_dot(kc_ref[...], q)
    if n_lat:
        scores(0, 0)
    m = jnp.max(s, axis=0, keepdims=True)
    if has_sink:
        head0 = pl.program_id(1) * group
        sink = jnp.concatenate(
            [jnp.full((1, tq), sink_ref[head0 + g] * LOG2E, F32) for g in range(group)], axis=1)
        m = jnp.maximum(m, sink)
    p = jnp.exp2(s - m)
    l = jnp.sum(p, axis=0, keepdims=True)
    if has_sink:
        l = l + jnp.exp2(sink - m)
    acc = jnp.dot(vc_ref[...], p.astype(BF16), preferred_element_type=F32)

    if n_lat:
        def update(t, slot, m, l, acc):
            s = s_ref[slot]
            m_new = jnp.maximum(m, jnp.max(s, axis=0, keepdims=True))
            alpha = jnp.exp2(m - m_new)
            p = jnp.exp2(s - m_new)
            l = alpha * l + jnp.sum(p, axis=0, keepdims=True)
            r0 = pl.multiple_of(t * tk, tk)
            acc = alpha * acc + jnp.dot(vl_ref[:, pl.ds(r0, tk)], p.astype(BF16), preferred_element_type=F32)
            return m_new, l, acc

        def pair(tt, carry):
            t0 = 2 * tt
            scores(t0 + 1, 1)
            carry = update(t0, 0, *carry)
            scores(jnp.minimum(t0 + 2, n_lat - 1), 0)
            return update(t0 + 1, 1, *carry)

        m, l, acc = lax.fori_loop(0, n_lat // 2, pair, (m, l, acc), unroll=unroll)
        if n_lat % 2:
            m, l, acc = update(n_lat - 1, 0, m, l, acc)

    o = (acc / l).T.astype(BF16)
    for g in range(group):
        o_ref[:, g * dv:(g + 1) * dv] = o[g * tq:(g + 1) * tq]


def _flash_call(q_arr, kv_ctx, kv_lat, sink, *, batch, n_q, n_ctx, n_heads, group, dq, dv, q_col, k_col, v_col, tq):
    n_kv = n_heads // group
    nqb = n_q // tq
    has_sink = sink is not None
    in_specs = [
        pl.BlockSpec((tq, group * dq), lambda b, g, i: (b * nqb + i, q_col // group + g)),
        pl.BlockSpec((n_ctx, dq), lambda b, g, i: (b, k_col + g)),
        pl.BlockSpec((dv, n_ctx), lambda b, g, i: (v_col + g, b)),
    ]
    args = [q_arr, kv_ctx[0], kv_ctx[1]]
    n_lat = 0
    tk = 0
    if kv_lat is not None:
        n_rows = kv_lat[0].shape[0] // batch
        tk = min(512, n_rows)
        n_lat = n_rows // tk
        in_specs += [
            pl.BlockSpec((n_rows, dq), lambda b, g, i: (b, k_col + g)),
            pl.BlockSpec((dv, n_rows), lambda b, g, i: (v_col + g, b)),
        ]
        args += [kv_lat[0], kv_lat[1]]
    if has_sink:
        in_specs.append(pl.BlockSpec(memory_space=pltpu.SMEM))
        args.append(sink)
    return pl.pallas_call(
        functools.partial(_flash_kernel, group=group, dq=dq, dv=dv, tk=tk, n_lat=n_lat, has_sink=has_sink,
                          unroll=max(1, min(FLASH_UNROLL, n_lat // 2))),
        grid=(batch, n_kv, nqb),
        scratch_shapes=[pltpu.VMEM((2, tk, group * tq), F32)] if n_lat else [],
        in_specs=in_specs,
        out_specs=pl.BlockSpec((tq, group * dv), lambda b, g, i: (b * nqb + i, g)),
        out_shape=jax.ShapeDtypeStruct((batch * n_q, n_heads * dv), BF16),
        compiler_params=_params("parallel", "parallel", "arbitrary"),
        name="flash",
    )(*args)


def _window_kernel(q_ref, kp_ref, kc_ref, kn_ref, vp_ref, vc_ref, vn_ref, kx_ref, vx_ref, sink_ref, o_ref, *, nb):
    i = pl.program_id(2)
    last = pl.num_programs(2) - 1
    tq = nb * Q_BLOCK
    q = jnp.concatenate([q_ref[:, g * HEAD_DIM:(g + 1) * HEAD_DIM] for g in range(GROUP_B)], axis=0)
    cols = GROUP_B * tq

    def qpos(rows):
        return lax.broadcasted_iota(jnp.int32, (rows, cols), 1) % tq

    def kpos(rows):
        return lax.broadcasted_iota(jnp.int32, (rows, cols), 0)

    off_p = jnp.where(i > 0, 0, tq + Q_BLOCK)
    off_n = jnp.where(i < last, 0, tq + Q_BLOCK)
    r_e, c_e = qpos(Q_BLOCK), kpos(Q_BLOCK)
    s_p = jnp.where(r_e + off_p <= c_e, _nt_dot(kp_ref[...], q), NEG_INF)
    s_n = jnp.where(r_e >= c_e + (tq - WINDOW) + off_n, _nt_dot(kn_ref[...], q), NEG_INF)
    s_c = jnp.where(jnp.abs(qpos(tq) - kpos(tq)) <= WINDOW, _nt_dot(kc_ref[...], q), NEG_INF)
    s_x = _nt_dot(kx_ref[...], q)
    head0 = pl.program_id(1) * GROUP_B
    sink = jnp.concatenate([jnp.full((1, tq), sink_ref[head0 + g] * LOG2E, F32) for g in range(GROUP_B)], axis=1)
    m = jnp.maximum(jnp.maximum(jnp.max(s_p, 0, keepdims=True), jnp.max(s_c, 0, keepdims=True)),
                    jnp.maximum(jnp.max(s_n, 0, keepdims=True), jnp.max(s_x, 0, keepdims=True)))
    m = jnp.maximum(m, sink)
    p_p, p_c, p_n, p_x = jnp.exp2(s_p - m), jnp.exp2(s_c - m), jnp.exp2(s_n - m), jnp.exp2(s_x - m)
    l = (jnp.sum(p_p, 0, keepdims=True) + jnp.sum(p_c, 0, keepdims=True)
         + jnp.sum(p_n, 0, keepdims=True) + jnp.sum(p_x, 0, keepdims=True) + jnp.exp2(sink - m))
    acc = (jnp.dot(vp_ref[...], p_p.astype(BF16), preferred_element_type=F32)
           + jnp.dot(vc_ref[...], p_c.astype(BF16), preferred_element_type=F32)
           + jnp.dot(vn_ref[...], p_n.astype(BF16), preferred_element_type=F32)
           + jnp.dot(vx_ref[...], p_x.astype(BF16), preferred_element_type=F32))
    o = (acc / l).T.astype(BF16)
    for g in range(GROUP_B):
        o_ref[:, g * HEAD_DIM:(g + 1) * HEAD_DIM] = o[g * tq:(g + 1) * tq]


def _window_call(qkv_x, vt_x, qkv_c, vt_c, sink, *, batch, n, n_ctx):
    assert WINDOW == Q_BLOCK
    nb = WINDOW_QBLOCKS if n % (WINDOW_QBLOCKS * Q_BLOCK) == 0 else 1
    tq = nb * Q_BLOCK
    steps = n // tq
    nblk = n // Q_BLOCK

    def edge_block(i, shift):
        return jnp.clip(i * nb + shift, 0, nblk - 1)

    in_specs = [
        pl.BlockSpec((tq, GROUP_B * HEAD_DIM), lambda b, g, i: (b * steps + i, COL_QB // GROUP_B + g)),
        pl.BlockSpec((Q_BLOCK, HEAD_DIM), lambda b, g, i: (b * nblk + edge_block(i, -1), COL_KB + g)),
        pl.BlockSpec((tq, HEAD_DIM), lambda b, g, i: (b * steps + i, COL_KB + g)),
        pl.BlockSpec((Q_BLOCK, HEAD_DIM), lambda b, g, i: (b * nblk + edge_block(i, nb), COL_KB + g)),
        pl.BlockSpec((HEAD_DIM, Q_BLOCK), lambda b, g, i: (N_KV_A + g, b * nblk + edge_block(i, -1))),
        pl.BlockSpec((HEAD_DIM, tq), lambda b, g, i: (N_KV_A + g, b * steps + i)),
        pl.BlockSpec((HEAD_DIM, Q_BLOCK), lambda b, g, i: (N_KV_A + g, b * nblk + edge_block(i, nb))),
        pl.BlockSpec((n_ctx, HEAD_DIM), lambda b, g, i: (b, COL_KB + g)),
        pl.BlockSpec((HEAD_DIM, n_ctx), lambda b, g, i: (N_KV_A + g, b)),
        pl.BlockSpec(memory_space=pltpu.SMEM),
    ]
    return pl.pallas_call(
        functools.partial(_window_kernel, nb=nb),
        grid=(batch, N_KV_B, steps),
        in_specs=in_specs,
        out_specs=pl.BlockSpec((tq, GROUP_B * HEAD_DIM), lambda b, g, i: (b * steps + i, g)),
        out_shape=jax.ShapeDtypeStruct((batch * n, QB), BF16),
        compiler_params=_params("parallel", "parallel", "arbitrary"),
        name="window",
    )(qkv_x, qkv_x, qkv_x, qkv_x, vt_x, vt_x, vt_x, qkv_c, vt_c, sink)


def _out_kernel(*refs, n_in):
    x_ref, mod_ref = refs[:2]
    o_ref = refs[2 + 2 * n_in]
    y = None
    for k in range(n_in):
        part = jnp.dot(refs[2 + 2 * k][...], refs[3 + 2 * k][...], preferred_element_type=F32)
        y = part if y is None else y + part
    o_ref[...] = x_ref[...] + _mod_row(mod_ref, 5) * y


def _out_call(x, mod4, pairs, *, l, mod_base, rows_per_mod):
    t, d = x.shape
    tm = min(512, t)
    in_specs = [
        pl.BlockSpec((tm, d), lambda i: (i, 0)),
        pl.BlockSpec((1, 1, N_MOD, d), lambda i: (l, mod_base + (i * tm) // rows_per_mod, 0, 0)),
    ]
    args = [x, mod4]
    for o, w in pairs:
        kk = o.shape[1]
        in_specs += [
            pl.BlockSpec((tm, kk), lambda i: (i, 0)),
            pl.BlockSpec((kk, d), lambda i: (0, 0), pipeline_mode=pl.Buffered(1)),
        ]
        args += [o, w]
    return pl.pallas_call(
        functools.partial(_out_kernel, n_in=len(pairs)),
        grid=(t // tm,),
        in_specs=in_specs,
        out_specs=pl.BlockSpec((tm, d), lambda i: (i, 0)),
        out_shape=jax.ShapeDtypeStruct((t, d), F32),
        compiler_params=_params("parallel"),
        name="out_proj",
    )(*args)


def _mla_weight_layout(w_in_c, w_uq, w_ukv):
    d = w_in_c.shape[0]
    w_in = jnp.concatenate([w_in_c, jnp.zeros((d, C_IN_PAD - w_in_c.shape[1]), w_in_c.dtype)], axis=1)
    wq = w_uq.reshape(MLA_Q_RANK, MLA_HEADS, MLA_NOPE + MLA_ROPE)
    wq = jnp.concatenate([wq, jnp.zeros((MLA_Q_RANK, MLA_HEADS, MLA_QK - MLA_NOPE - MLA_ROPE), w_uq.dtype)], axis=-1)
    wkv = w_ukv.reshape(MLA_KV_RANK, MLA_HEADS, MLA_NOPE + MLA_V)
    wk = wkv[:, :, :MLA_NOPE].reshape(MLA_KV_RANK, MLA_HEADS * MLA_NOPE)
    wvt = wkv[:, :, MLA_NOPE:].reshape(MLA_KV_RANK, MLA_HEADS * MLA_V).T
    return (w_in.astype(BF16), wq.reshape(MLA_Q_RANK, MLA_HEADS * MLA_QK).astype(BF16), wk.astype(BF16),
            wvt.astype(BF16))


def kernel(x, c, ctx, c_ctx, w_mod, b_mod, g_norm, w_gate_up, w_down, w_in_ab, g_qnorm_a, g_knorm_a,
           sink_b, w_out_ab, w_in_c, g_cq, g_ckv, w_uq, w_ukv, w_out_c, g_final):
    batch, n, d = x.shape
    n_ctx = ctx.shape[1]
    depth = w_mod.shape[0]
    assert batch + 1 <= 8 and n % GRID_W == 0

    cc = jnp.concatenate([c, c_ctx[None], jnp.zeros((8 - batch - 1, d), F32)], axis=0)
    mod4 = _mod_call(cc, w_mod, b_mod)

    w_gu = w_gate_up.astype(BF16)
    w_dn = w_down.astype(BF16)
    w_ab = w_in_ab.astype(BF16)
    w_oab = w_out_ab.astype(BF16)
    w_oc = w_out_c.astype(BF16)

    cos_h, sin_h = _rope_tables(n, HEAD_DIM)
    cos_m, sin_m = _rope_tables(n, MLA_ROPE)
    pad = jnp.zeros((n, 128 - MLA_ROPE), F32)
    tables_h = (cos_h, sin_h)
    tables_m = (jnp.concatenate([cos_m, pad], axis=1), jnp.concatenate([sin_m, pad], axis=1))

    xs = x.reshape(batch * n, d)
    hs = ctx.reshape(batch * n_ctx, d)
    x_mod = dict(mod_base=0, rows_per_mod=n)
    c_mod = dict(mod_base=batch, rows_per_mod=batch * n_ctx)

    for l in range(depth):
        need_ctx = l < depth - 1
        last = l == depth - 1
        i = l // 2
        xs = _ffn_call(xs, mod4, g_norm, w_gu, w_dn, l=l, jf=0, jm=0, **x_mod)
        hs = _ffn_call(hs, mod4, g_norm, w_gu, w_dn, l=l, jf=0, jm=0, **c_mod)
        if l % 2 == 0:
            w_vt = jnp.concatenate([w_ab[i][:, COL_VA * HEAD_DIM:COL_QB * HEAD_DIM],
                                    w_ab[i][:, COL_VB * HEAD_DIM:]], axis=1).T
            qkv_x, vt_x = _proj_ab_call(xs, mod4, g_norm, w_ab[i], w_vt, g_qnorm_a[i], g_knorm_a[i], tables_h,
                                        l=l, n_pos=n, **x_mod)
            qkv_c, vt_c = _proj_ab_call(hs, mod4, g_norm, w_ab[i], w_vt, g_qnorm_a[i], g_knorm_a[i], None,
                                        l=l, n_pos=n, **c_mod)
            oa = _flash_call(qkv_x, (qkv_c, vt_c), (qkv_x, vt_x), None, batch=batch, n_q=n, n_ctx=n_ctx,
                             n_heads=N_HEADS_A, group=GROUP_A, dq=HEAD_DIM, dv=HEAD_DIM,
                             q_col=COL_QA, k_col=COL_KA, v_col=0, tq=min(256, n))
            ob = _window_call(qkv_x, vt_x, qkv_c, vt_c, sink_b[i], batch=batch, n=n, n_ctx=n_ctx)
            w_o = w_oab[i]
            xs_new = _out_call(xs, mod4, [(oa, w_o[:QA]), (ob, w_o[QA:])], l=l, **x_mod)
            if need_ctx:
                oa_c = _flash_call(qkv_c, (qkv_c, vt_c), None, None, batch=batch, n_q=n_ctx, n_ctx=n_ctx,
                                   n_heads=N_HEADS_A, group=GROUP_A, dq=HEAD_DIM, dv=HEAD_DIM,
                                   q_col=COL_QA, k_col=COL_KA, v_col=0, tq=min(128, n_ctx))
                ob_c = _flash_call(qkv_c, (qkv_c, vt_c), None, sink_b[i], batch=batch, n_q=n_ctx, n_ctx=n_ctx,
                                   n_heads=N_HEADS_B, group=GROUP_B, dq=HEAD_DIM, dv=HEAD_DIM,
                                   q_col=COL_QB, k_col=COL_KB, v_col=N_KV_A, tq=min(128, n_ctx))
                hs = _out_call(hs, mod4, [(oa_c, w_o[:QA]), (ob_c, w_o[QA:])], l=l, **c_mod)
            xs = xs_new
        else:
            w_in, wq, wk, wvt = _mla_weight_layout(w_in_c[i], w_uq[i], w_ukv[i])
            q_x, k_x, v_x = _proj_mla_call(xs, mod4, g_norm, w_in, g_cq[i], g_ckv[i], wq, wk, wvt, tables_m,
                                           l=l, n_pos=n, **x_mod)
            q_c, k_c, v_c = _proj_mla_call(hs, mod4, g_norm, w_in, g_cq[i], g_ckv[i], wq, wk, wvt, None,
                                           l=l, n_pos=n, **c_mod)
            o = _flash_call(q_x, (k_c, v_c), (k_x, v_x), None, batch=batch, n_q=n, n_ctx=n_ctx,
                            n_heads=MLA_HEADS, group=1, dq=MLA_QK, dv=MLA_V, q_col=0, k_col=0, v_col=0,
                            tq=min(1024, n))
            xs_new = _out_call(xs, mod4, [(o, w_oc[i])], l=l, **x_mod)
            if need_ctx:
                o_c = _flash_call(q_c, (k_c, v_c), None, None, batch=batch, n_q=n_ctx, n_ctx=n_ctx,
                                  n_heads=MLA_HEADS, group=1, dq=MLA_QK, dv=MLA_V, q_col=0, k_col=0, v_col=0,
                                  tq=n_ctx)
                hs = _out_call(hs, mod4, [(o_c, w_oc[i])], l=l, **c_mod)
            xs = xs_new
        xs = _ffn_call(xs, mod4, g_norm, w_gu, w_dn, l=l, jf=1, jm=2, g_final=g_final if last else None, **x_mod)
        if need_ctx:
            hs = _ffn_call(hs, mod4, g_norm, w_gu, w_dn, l=l, jf=1, jm=2, **c_mod)
    return xs.reshape(batch, n, d)
```

```python
import functools

import jax
import jax.numpy as jnp
from jax import lax
from jax.experimental import pallas as pl
from jax.experimental.pallas import tpu as pltpu

GRID_W = 64
HEAD_DIM = 128
N_HEADS_A = 8
N_KV_A = 2
N_HEADS_B = 8
N_KV_B = 2
WINDOW = 128
Q_BLOCK = 128
ROPE_THETA = 10000.0
MLA_HEADS = 16
MLA_Q_RANK = 512
MLA_KV_RANK = 256
MLA_NOPE = 128
MLA_ROPE = 64
MLA_V = 128
MLA_QK = 256
N_MOD = 9
EPS = 1e-6
NEG_INF = -1e30
LOG2E = 1.4426950408889634

QA = N_HEADS_A * HEAD_DIM
KA = N_KV_A * HEAD_DIM
QB = N_HEADS_B * HEAD_DIM
KB = N_KV_B * HEAD_DIM
AB_IN = QA + 2 * KA + QB + 2 * KB
GROUP_A = N_HEADS_A // N_KV_A
GROUP_B = N_HEADS_B // N_KV_B
COL_QA = 0
COL_KA = N_HEADS_A
COL_VA = COL_KA + N_KV_A
COL_QB = COL_VA + N_KV_A
COL_KB = COL_QB + N_HEADS_B
COL_VB = COL_KB + N_KV_B
N_COLS_AB = COL_VB + N_KV_B
C_IN_PAD = MLA_Q_RANK + MLA_KV_RANK + 128

F32 = jnp.float32
BF16 = jnp.bfloat16
V7X_VMEM_LIMIT_BYTES = 56 * 1024 * 1024
FLASH_UNROLL = 4
WINDOW_QBLOCKS = 2
ADALN_ROWS = 16
ADALN_UNROLL = 8
ONES_ROWS = 16
VT_ROWS = HEAD_DIM + ONES_ROWS


def _params(*sem):
    return pltpu.CompilerParams(dimension_semantics=sem, vmem_limit_bytes=V7X_VMEM_LIMIT_BYTES)


def _rms(x):
    return x * lax.rsqrt(jnp.mean(x * x, axis=-1, keepdims=True) + EPS)


def _silu(x):
    return x / (1.0 + jnp.exp(-x))


def _mod_row(mod_ref, k):
    return mod_ref[0, 0, k:k + 1, :]


def _adaln(x, g_ref, jn, mod_ref, j):
    return _rms(x) * g_ref[0, jn:jn + 1, :] * (1.0 + _mod_row(mod_ref, 3 * j + 1)) + _mod_row(mod_ref, 3 * j)


def _adaln_rows(x_ref, out_ref, g_ref, jn, mod_ref, j):
    rows = ADALN_ROWS
    gain = g_ref[0, jn:jn + 1, :] * (1.0 + _mod_row(mod_ref, 3 * j + 1))
    shift = _mod_row(mod_ref, 3 * j)

    def body(r, carry):
        r0 = pl.multiple_of(r * rows, rows)
        x = x_ref[pl.ds(r0, rows), :]
        out_ref[pl.ds(r0, rows), :] = (_rms(x) * gain + shift).astype(BF16)
        return carry

    n_groups = x_ref.shape[0] // rows
    lax.fori_loop(0, n_groups, body, 0, unroll=min(ADALN_UNROLL, n_groups))


def _mod_kernel(c_ref, w_ref, b_ref, o_ref):
    sc = _silu(c_ref[...]).astype(BF16)
    o_ref[0] = jnp.dot(sc, w_ref[0].astype(BF16), preferred_element_type=F32) + b_ref[0]


def _mod_call(cc, w_mod, b_mod):
    depth, d, nd = w_mod.shape
    tn = min(d, 1024)
    out = pl.pallas_call(
        _mod_kernel,
        grid=(depth, nd // tn),
        in_specs=[
            pl.BlockSpec((8, d), lambda l, j: (0, 0)),
            pl.BlockSpec((1, d, tn), lambda l, j: (l, 0, j)),
            pl.BlockSpec((1, 1, tn), lambda l, j: (l, 0, j)),
        ],
        out_specs=pl.BlockSpec((1, 8, tn), lambda l, j: (l, 0, j)),
        out_shape=jax.ShapeDtypeStruct((depth, 8, nd), F32),
        compiler_params=_params("parallel", "parallel"),
        name="mod",
    )(cc, w_mod, b_mod.reshape(depth, 1, nd))
    return out.reshape(depth, 8, N_MOD, d)


def _ffn_kernel(*refs, jm, jn, final):
    if final:
        x_ref, mod_ref, g_ref, wg_ref, wu_ref, wd_ref, gf_ref, o_ref, xn_ref = refs
    else:
        x_ref, mod_ref, g_ref, wg_ref, wu_ref, wd_ref, o_ref, xn_ref = refs
    j = pl.program_id(1)

    @pl.when(j == 0)
    def _():
        _adaln_rows(x_ref, xn_ref, g_ref, jn, mod_ref, jm)
        o_ref[...] = jnp.zeros_like(o_ref)

    xn = xn_ref[...]
    gt = jnp.dot(xn, wg_ref[0, 0], preferred_element_type=F32)
    up = jnp.dot(xn, wu_ref[0, 0], preferred_element_type=F32)
    h = (_silu(gt) * up).astype(BF16)
    o_ref[...] += jnp.dot(h, wd_ref[0, 0], preferred_element_type=F32)

    @pl.when(j == pl.num_programs(1) - 1)
    def _():
        y = x_ref[...] + (0.5 * _mod_row(mod_ref, 3 * jm + 2)) * o_ref[...]
        if final:
            y = _rms(y) * gf_ref[...]
        o_ref[...] = y


def _ffn_call(x, mod4, g_norm, w_gu, w_dn, *, l, jf, jm, mod_base, rows_per_mod, g_final=None):
    t, d = x.shape
    f = w_dn.shape[2]
    tm = min(512, t)
    tf = min(512, f)
    nf = f // tf
    final = g_final is not None
    in_specs = [
        pl.BlockSpec((tm, d), lambda i, j: (i, 0)),
        pl.BlockSpec((1, 1, N_MOD, d), lambda i, j: (l, mod_base + (i * tm) // rows_per_mod, 0, 0)),
        pl.BlockSpec((1, 3, d), lambda i, j: (l, 0, 0)),
        pl.BlockSpec((1, 1, d, tf), lambda i, j: (l, jf, 0, j)),
        pl.BlockSpec((1, 1, d, tf), lambda i, j: (l, jf, 0, nf + j)),
        pl.BlockSpec((1, 1, tf, d), lambda i, j: (l, jf, j, 0)),
    ]
    args = [x, mod4, g_norm, w_gu, w_gu, w_dn]
    if final:
        in_specs.append(pl.BlockSpec((1, d), lambda i, j: (0, 0)))
        args.append(g_final.reshape(1, d))
    return pl.pallas_call(
        functools.partial(_ffn_kernel, jm=jm, jn=jm, final=final),
        grid=(t // tm, nf),
        in_specs=in_specs,
        out_specs=pl.BlockSpec((tm, d), lambda i, j: (i, 0)),
        out_shape=jax.ShapeDtypeStruct((t, d), F32),
        scratch_shapes=[pltpu.VMEM((tm, d), BF16)],
        compiler_params=_params("parallel", "arbitrary"),
        name="ffn",
    )(*args)


def _rope_tables(n_pos, rot_dim):
    nf = rot_dim // 4
    pos = jnp.arange(n_pos)
    inv = ROPE_THETA ** (-jnp.arange(nf, dtype=F32) / nf)
    ang_r = (pos // GRID_W).astype(F32)[:, None] * inv
    ang_c = (pos % GRID_W).astype(F32)[:, None] * inv
    cos = jnp.concatenate([jnp.cos(ang_r), jnp.cos(ang_r), jnp.cos(ang_c), jnp.cos(ang_c)], axis=-1)
    sin = jnp.concatenate([-jnp.sin(ang_r), jnp.sin(ang_r), -jnp.sin(ang_c), jnp.sin(ang_c)], axis=-1)
    return cos, sin


def _rotate(y, cos, sin, nf):
    lanes = y.shape[-1]
    lane = lax.broadcasted_iota(jnp.int32, y.shape, 1)
    partner = jnp.where(lane % (2 * nf) < nf, pltpu.roll(y, lanes - nf, 1), pltpu.roll(y, nf, 1))
    return y * cos + partner * sin


def _store_vt(vt_ref, vt, n_heads):
    ones = jnp.ones((ONES_ROWS, vt.shape[1]), BF16)
    for h in range(n_heads):
        vt_ref[h * VT_ROWS:h * VT_ROWS + HEAD_DIM, :] = vt[h * HEAD_DIM:(h + 1) * HEAD_DIM].astype(BF16)
        vt_ref[h * VT_ROWS + HEAD_DIM:(h + 1) * VT_ROWS, :] = ones


def _proj_ab_kernel(*refs, rope):
    if rope:
        x_ref, mod_ref, g_ref, w_ref, wvt_ref, gq_ref, gk_ref, cos_ref, sin_ref, o_ref, vt_ref = refs
        cos, sin = cos_ref[...], sin_ref[...]
    else:
        x_ref, mod_ref, g_ref, w_ref, wvt_ref, gq_ref, gk_ref, o_ref, vt_ref = refs
    xn = _adaln(x_ref[...], g_ref, 1, mod_ref, 1).astype(BF16)
    _store_vt(vt_ref, _nt_dot(wvt_ref[...], xn), N_KV_A + N_KV_B)
    scale = HEAD_DIM ** -0.5 * LOG2E
    for c in range(N_COLS_AB // 2):
        y2 = jnp.dot(xn, w_ref[:, c * 256:(c + 1) * 256], preferred_element_type=F32)
        for hh in range(2):
            col = 2 * c + hh
            y = y2[:, hh * HEAD_DIM:(hh + 1) * HEAD_DIM]
            is_qa = col < COL_KA
            is_ka = COL_KA <= col < COL_VA
            is_qb = COL_QB <= col < COL_KB
            is_kb = COL_KB <= col < COL_VB
            if is_qa:
                y = _rms(y) * gq_ref[...]
            if is_ka:
                y = _rms(y) * gk_ref[...]
            if rope and (is_qa or is_ka or is_qb or is_kb):
                y = _rotate(y, cos, sin, HEAD_DIM // 4)
            if is_qa or is_qb:
                y = y * scale
            o_ref[:, col * HEAD_DIM:(col + 1) * HEAD_DIM] = y.astype(BF16)


def _proj_ab_call(x, mod4, g_norm, w_in, w_vt, g_q, g_k, tables, *, l, mod_base, rows_per_mod, n_pos):
    t, d = x.shape
    tm = min(256, t)
    rope = tables is not None
    const = dict(pipeline_mode=pl.Buffered(1))
    in_specs = [
        pl.BlockSpec((tm, d), lambda i: (i, 0)),
        pl.BlockSpec((1, 1, N_MOD, d), lambda i: (l, mod_base + (i * tm) // rows_per_mod, 0, 0)),
        pl.BlockSpec((1, 3, d), lambda i: (l, 0, 0)),
        pl.BlockSpec((d, AB_IN), lambda i: (0, 0), **const),
        pl.BlockSpec((KA + KB, d), lambda i: (0, 0), **const),
        pl.BlockSpec((1, HEAD_DIM), lambda i: (0, 0)),
        pl.BlockSpec((1, HEAD_DIM), lambda i: (0, 0)),
    ]
    args = [x, mod4, g_norm, w_in, w_vt, g_q.reshape(1, HEAD_DIM), g_k.reshape(1, HEAD_DIM)]
    if rope:
        npb = n_pos // tm
        in_specs += [pl.BlockSpec((tm, HEAD_DIM), lambda i: (i % npb, 0))] * 2
        args += list(tables)
    return pl.pallas_call(
        functools.partial(_proj_ab_kernel, rope=rope),
        grid=(t // tm,),
        in_specs=in_specs,
        out_specs=[pl.BlockSpec((tm, AB_IN), lambda i: (i, 0)),
                   pl.BlockSpec(((N_KV_A + N_KV_B) * VT_ROWS, tm), lambda i: (0, i))],
        out_shape=[jax.ShapeDtypeStruct((t, AB_IN), BF16),
                   jax.ShapeDtypeStruct(((N_KV_A + N_KV_B) * VT_ROWS, t), BF16)],
        compiler_params=_params("parallel"),
        name="proj_ab",
    )(*args)


def _proj_mla_kernel(*refs, rope):
    if rope:
        (x_ref, mod_ref, g_ref, wi_ref, gcq_ref, gckv_ref, wuq_ref, wuk_ref, wuvt_ref, cos_ref, sin_ref,
         q_ref, k_ref, vt_ref) = refs
        cos, sin = cos_ref[...], sin_ref[...]
    else:
        x_ref, mod_ref, g_ref, wi_ref, gcq_ref, gckv_ref, wuq_ref, wuk_ref, wuvt_ref, q_ref, k_ref, vt_ref = refs
    xn = _adaln(x_ref[...], g_ref, 1, mod_ref, 1).astype(BF16)
    t = jnp.dot(xn, wi_ref[...], preferred_element_type=F32)
    cq = (_rms(t[:, :MLA_Q_RANK]) * gcq_ref[...]).astype(BF16)
    ckv = (_rms(t[:, MLA_Q_RANK:MLA_Q_RANK + MLA_KV_RANK]) * gckv_ref[...]).astype(BF16)
    kr = t[:, MLA_Q_RANK + MLA_KV_RANK:]
    if rope:
        kr = _rotate(kr, cos, sin, MLA_ROPE // 4)
    kr = kr.astype(BF16)
    _store_vt(vt_ref, _nt_dot(wuvt_ref[...], ckv), MLA_HEADS)
    scale = (MLA_NOPE + MLA_ROPE) ** -0.5 * LOG2E
    for h in range(MLA_HEADS):
        q = jnp.dot(cq, wuq_ref[:, h * MLA_QK:(h + 1) * MLA_QK], preferred_element_type=F32)
        qn, qr = q[:, :MLA_NOPE], q[:, MLA_NOPE:]
        if rope:
            qr = _rotate(qr, cos, sin, MLA_ROPE // 4)
        q_ref[:, h * MLA_QK:h * MLA_QK + MLA_NOPE] = (qn * scale).astype(BF16)
        q_ref[:, h * MLA_QK + MLA_NOPE:(h + 1) * MLA_QK] = (qr * scale).astype(BF16)
        k_ref[:, h * MLA_QK + MLA_NOPE:(h + 1) * MLA_QK] = kr
    for hp in range(MLA_HEADS // 2):
        kn = jnp.dot(ckv, wuk_ref[:, hp * 2 * MLA_NOPE:(hp + 1) * 2 * MLA_NOPE], preferred_element_type=F32)
        for hh in range(2):
            h = 2 * hp + hh
            k_ref[:, h * MLA_QK:h * MLA_QK + MLA_NOPE] = kn[:, hh * MLA_NOPE:(hh + 1) * MLA_NOPE].astype(BF16)


def _proj_mla_call(x, mod4, g_norm, w_in, g_cq, g_ckv, w_uq, w_uk, w_uvt, tables, *, l, mod_base, rows_per_mod,
                   n_pos):
    t, d = x.shape
    tm = min(256, t)
    rope = tables is not None
    const = dict(pipeline_mode=pl.Buffered(1))
    in_specs = [
        pl.BlockSpec((tm, d), lambda i: (i, 0)),
        pl.BlockSpec((1, 1, N_MOD, d), lambda i: (l, mod_base + (i * tm) // rows_per_mod, 0, 0)),
        pl.BlockSpec((1, 3, d), lambda i: (l, 0, 0)),
        pl.BlockSpec((d, C_IN_PAD), lambda i: (0, 0), **const),
        pl.BlockSpec((1, MLA_Q_RANK), lambda i: (0, 0)),
        pl.BlockSpec((1, MLA_KV_RANK), lambda i: (0, 0)),
        pl.BlockSpec((MLA_Q_RANK, MLA_HEADS * MLA_QK), lambda i: (0, 0), **const),
        pl.BlockSpec((MLA_KV_RANK, MLA_HEADS * MLA_NOPE), lambda i: (0, 0), **const),
        pl.BlockSpec((MLA_HEADS * MLA_V, MLA_KV_RANK), lambda i: (0, 0), **const),
    ]
    args = [x, mod4, g_norm, w_in, g_cq.reshape(1, -1), g_ckv.reshape(1, -1), w_uq, w_uk, w_uvt]
    if rope:
        npb = n_pos // tm
        in_specs += [pl.BlockSpec((tm, 128), lambda i: (i % npb, 0))] * 2
        args += list(tables)
    return pl.pallas_call(
        functools.partial(_proj_mla_kernel, rope=rope),
        grid=(t // tm,),
        in_specs=in_specs,
        out_specs=[
            pl.BlockSpec((tm, MLA_HEADS * MLA_QK), lambda i: (i, 0)),
            pl.BlockSpec((tm, MLA_HEADS * MLA_QK), lambda i: (i, 0)),
            pl.BlockSpec((MLA_HEADS * VT_ROWS, tm), lambda i: (0, i)),
        ],
        out_shape=[
            jax.ShapeDtypeStruct((t, MLA_HEADS * MLA_QK), BF16),
            jax.ShapeDtypeStruct((t, MLA_HEADS * MLA_QK), BF16),
            jax.ShapeDtypeStruct((MLA_HEADS * VT_ROWS, t), BF16),
        ],
        compiler_params=_params("parallel"),
        name="proj_mla",
    )(*args)


def _nt_dot(a, b):
    return lax.dot_general(a, b, (((1,), (1,)), ((), ())), preferred_element_type=F32)


def _flash_kernel(*refs, group, dq, dv, tk, n_lat, has_sink, unroll):
    refs = list(refs)
    q_ref, kc_ref, vc_ref = refs[:3]
    pos = 3
    if n_lat:
        kl_ref, vl_ref = refs[pos:pos + 2]
        pos += 2
    if has_sink:
        sink_ref = refs[pos]
        pos += 1
    o_ref = refs[pos]
    if n_lat:
        s_ref = refs[pos + 1]
    tq = q_ref.shape[0]
    q = jnp.concatenate([q_ref[:, g * dq:(g + 1) * dq] for g in range(group)], axis=0) if group > 1 else q_ref[...]

    def scores(t, slot):
        s_ref[slot] = _nt_dot(kl_ref[pl.ds(pl.multiple_of(t * tk, tk), tk), :], q)

    s = _nt_dot(kc_ref[...], q)
    if n_lat:
        scores(0, 0)
    m = jnp.max(s, axis=0, keepdims=True)
    if has_sink:
        head0 = pl.program_id(1) * group
        sink = jnp.concatenate(
            [jnp.full((1, tq), sink_ref[head0 + g] * LOG2E, F32) for g in range(group)], axis=1)
        m = jnp.maximum(m, sink)
    p = jnp.exp2(s - m)
    acc = jnp.dot(vc_ref[...], p.astype(BF16), preferred_element_type=F32)

    if n_lat:
        def update(t, slot, m, acc):
            s = s_ref[slot]
            m_new = jnp.maximum(m, jnp.max(s, axis=0, keepdims=True))
            alpha = jnp.exp2(m - m_new)
            p = jnp.exp2(s - m_new)
            r0 = pl.multiple_of(t * tk, tk)
            acc = alpha * acc + jnp.dot(vl_ref[:, pl.ds(r0, tk)], p.astype(BF16), preferred_element_type=F32)
            return m_new, acc

        def pair(tt, carry):
            t0 = 2 * tt
            scores(t0 + 1, 1)
            carry = update(t0, 0, *carry)
            scores(jnp.minimum(t0 + 2, n_lat - 1), 0)
            return update(t0 + 1, 1, *carry)

        m, acc = lax.fori_loop(0, n_lat // 2, pair, (m, acc), unroll=unroll)
        if n_lat % 2:
            m, acc = update(n_lat - 1, 0, m, acc)

    l = acc[dv:dv + 1]
    if has_sink:
        l = l + jnp.exp2(sink - m)
    o = (acc[:dv] / l).T.astype(BF16)
    for g in range(group):
        o_ref[:, g * dv:(g + 1) * dv] = o[g * tq:(g + 1) * tq]


def _flash_call(q_arr, kv_ctx, kv_lat, sink, *, batch, n_q, n_ctx, n_heads, group, dq, dv, q_col, k_col, v_col, tq):
    n_kv = n_heads // group
    nqb = n_q // tq
    has_sink = sink is not None
    in_specs = [
        pl.BlockSpec((tq, group * dq), lambda b, g, i: (b * nqb + i, q_col // group + g)),
        pl.BlockSpec((n_ctx, dq), lambda b, g, i: (b, k_col + g)),
        pl.BlockSpec((dv + ONES_ROWS, n_ctx), lambda b, g, i: (v_col + g, b)),
    ]
    args = [q_arr, kv_ctx[0], kv_ctx[1]]
    n_lat = 0
    tk = 0
    if kv_lat is not None:
        n_rows = kv_lat[0].shape[0] // batch
        tk = min(512, n_rows)
        n_lat = n_rows // tk
        in_specs += [
            pl.BlockSpec((n_rows, dq), lambda b, g, i: (b, k_col + g)),
            pl.BlockSpec((dv + ONES_ROWS, n_rows), lambda b, g, i: (v_col + g, b)),
        ]
        args += [kv_lat[0], kv_lat[1]]
    if has_sink:
        in_specs.append(pl.BlockSpec(memory_space=pltpu.SMEM))
        args.append(sink)
    return pl.pallas_call(
        functools.partial(_flash_kernel, group=group, dq=dq, dv=dv, tk=tk, n_lat=n_lat, has_sink=has_sink,
                          unroll=max(1, min(FLASH_UNROLL, n_lat // 2))),
        grid=(batch, n_kv, nqb),
        scratch_shapes=[pltpu.VMEM((2, tk, group * tq), F32)] if n_lat else [],
        in_specs=in_specs,
        out_specs=pl.BlockSpec((tq, group * dv), lambda b, g, i: (b * nqb + i, g)),
        out_shape=jax.ShapeDtypeStruct((batch * n_q, n_heads * dv), BF16),
        compiler_params=_params("parallel", "parallel", "arbitrary"),
        name="flash",
    )(*args)


def _window_kernel(q_ref, kp_ref, kc_ref, kn_ref, vp_ref, vc_ref, vn_ref, kx_ref, vx_ref, sink_ref, o_ref, *, nb):
    i = pl.program_id(2)
    last = pl.num_programs(2) - 1
    tq = nb * Q_BLOCK
    q = jnp.concatenate([q_ref[:, g * HEAD_DIM:(g + 1) * HEAD_DIM] for g in range(GROUP_B)], axis=0)
    cols = GROUP_B * tq

    def qpos(rows):
        return lax.broadcasted_iota(jnp.int32, (rows, cols), 1) % tq

    def kpos(rows):
        return lax.broadcasted_iota(jnp.int32, (rows, cols), 0)

    off_p = jnp.where(i > 0, 0, tq + Q_BLOCK)
    off_n = jnp.where(i < last, 0, tq + Q_BLOCK)
    r_e, c_e = qpos(Q_BLOCK), kpos(Q_BLOCK)
    s_p = jnp.where(r_e + off_p <= c_e, _nt_dot(kp_ref[...], q), NEG_INF)
    s_n = jnp.where(r_e >= c_e + (tq - WINDOW) + off_n, _nt_dot(kn_ref[...], q), NEG_INF)
    s_c = jnp.where(jnp.abs(qpos(tq) - kpos(tq)) <= WINDOW, _nt_dot(kc_ref[...], q), NEG_INF)
    s_x = _nt_dot(kx_ref[...], q)
    head0 = pl.program_id(1) * GROUP_B
    sink = jnp.concatenate([jnp.full((1, tq), sink_ref[head0 + g] * LOG2E, F32) for g in range(GROUP_B)], axis=1)
    m = jnp.maximum(jnp.maximum(jnp.max(s_p, 0, keepdims=True), jnp.max(s_c, 0, keepdims=True)),
                    jnp.maximum(jnp.max(s_n, 0, keepdims=True), jnp.max(s_x, 0, keepdims=True)))
    m = jnp.maximum(m, sink)
    p_p, p_c, p_n, p_x = jnp.exp2(s_p - m), jnp.exp2(s_c - m), jnp.exp2(s_n - m), jnp.exp2(s_x - m)
    acc = (jnp.dot(vp_ref[...], p_p.astype(BF16), preferred_element_type=F32)
           + jnp.dot(vc_ref[...], p_c.astype(BF16), preferred_element_type=F32)
           + jnp.dot(vn_ref[...], p_n.astype(BF16), preferred_element_type=F32)
           + jnp.dot(vx_ref[...], p_x.astype(BF16), preferred_element_type=F32))
    l = acc[HEAD_DIM:HEAD_DIM + 1] + jnp.exp2(sink - m)
    o = (acc[:HEAD_DIM] / l).T.astype(BF16)
    for g in range(GROUP_B):
        o_ref[:, g * HEAD_DIM:(g + 1) * HEAD_DIM] = o[g * tq:(g + 1) * tq]


def _window_call(qkv_x, vt_x, qkv_c, vt_c, sink, *, batch, n, n_ctx):
    assert WINDOW == Q_BLOCK
    nb = WINDOW_QBLOCKS if n % (WINDOW_QBLOCKS * Q_BLOCK) == 0 else 1
    tq = nb * Q_BLOCK
    steps = n // tq
    nblk = n // Q_BLOCK

    def edge_block(i, shift):
        return jnp.clip(i * nb + shift, 0, nblk - 1)

    in_specs = [
        pl.BlockSpec((tq, GROUP_B * HEAD_DIM), lambda b, g, i: (b * steps + i, COL_QB // GROUP_B + g)),
        pl.BlockSpec((Q_BLOCK, HEAD_DIM), lambda b, g, i: (b * nblk + edge_block(i, -1), COL_KB + g)),
        pl.BlockSpec((tq, HEAD_DIM), lambda b, g, i: (b * steps + i, COL_KB + g)),
        pl.BlockSpec((Q_BLOCK, HEAD_DIM), lambda b, g, i: (b * nblk + edge_block(i, nb), COL_KB + g)),
        pl.BlockSpec((VT_ROWS, Q_BLOCK), lambda b, g, i: (N_KV_A + g, b * nblk + edge_block(i, -1))),
        pl.BlockSpec((VT_ROWS, tq), lambda b, g, i: (N_KV_A + g, b * steps + i)),
        pl.BlockSpec((VT_ROWS, Q_BLOCK), lambda b, g, i: (N_KV_A + g, b * nblk + edge_block(i, nb))),
        pl.BlockSpec((n_ctx, HEAD_DIM), lambda b, g, i: (b, COL_KB + g)),
        pl.BlockSpec((VT_ROWS, n_ctx), lambda b, g, i: (N_KV_A + g, b)),
        pl.BlockSpec(memory_space=pltpu.SMEM),
    ]
    return pl.pallas_call(
        functools.partial(_window_kernel, nb=nb),
        grid=(batch, N_KV_B, steps),
        in_specs=in_specs,
        out_specs=pl.BlockSpec((tq, GROUP_B * HEAD_DIM), lambda b, g, i: (b * steps + i, g)),
        out_shape=jax.ShapeDtypeStruct((batch * n, QB), BF16),
        compiler_params=_params("parallel", "parallel", "arbitrary"),
        name="window",
    )(qkv_x, qkv_x, qkv_x, qkv_x, vt_x, vt_x, vt_x, qkv_c, vt_c, sink)


def _out_kernel(*refs, n_in):
    x_ref, mod_ref = refs[:2]
    o_ref = refs[2 + 2 * n_in]
    y = None
    for k in range(n_in):
        part = jnp.dot(refs[2 + 2 * k][...], refs[3 + 2 * k][...], preferred_element_type=F32)
        y = part if y is None else y + part
    o_ref[...] = x_ref[...] + _mod_row(mod_ref, 5) * y


def _out_call(x, mod4, pairs, *, l, mod_base, rows_per_mod):
    t, d = x.shape
    tm = min(512, t)
    in_specs = [
        pl.BlockSpec((tm, d), lambda i: (i, 0)),
        pl.BlockSpec((1, 1, N_MOD, d), lambda i: (l, mod_base + (i * tm) // rows_per_mod, 0, 0)),
    ]
    args = [x, mod4]
    for o, w in pairs:
        kk = o.shape[1]
        in_specs += [
            pl.BlockSpec((tm, kk), lambda i: (i, 0)),
            pl.BlockSpec((kk, d), lambda i: (0, 0), pipeline_mode=pl.Buffered(1)),
        ]
        args += [o, w]
    return pl.pallas_call(
        functools.partial(_out_kernel, n_in=len(pairs)),
        grid=(t // tm,),
        in_specs=in_specs,
        out_specs=pl.BlockSpec((tm, d), lambda i: (i, 0)),
        out_shape=jax.ShapeDtypeStruct((t, d), F32),
        compiler_params=_params("parallel"),
        name="out_proj",
    )(*args)


def _mla_weight_layout(w_in_c, w_uq, w_ukv):
    d = w_in_c.shape[0]
    w_in = jnp.concatenate([w_in_c, jnp.zeros((d, C_IN_PAD - w_in_c.shape[1]), w_in_c.dtype)], axis=1)
    wq = w_uq.reshape(MLA_Q_RANK, MLA_HEADS, MLA_NOPE + MLA_ROPE)
    wq = jnp.concatenate([wq, jnp.zeros((MLA_Q_RANK, MLA_HEADS, MLA_QK - MLA_NOPE - MLA_ROPE), w_uq.dtype)], axis=-1)
    wkv = w_ukv.reshape(MLA_KV_RANK, MLA_HEADS, MLA_NOPE + MLA_V)
    wk = wkv[:, :, :MLA_NOPE].reshape(MLA_KV_RANK, MLA_HEADS * MLA_NOPE)
    wvt = wkv[:, :, MLA_NOPE:].reshape(MLA_KV_RANK, MLA_HEADS * MLA_V).T
    return (w_in.astype(BF16), wq.reshape(MLA_Q_RANK, MLA_HEADS * MLA_QK).astype(BF16), wk.astype(BF16),
            wvt.astype(BF16))


def kernel(x, c, ctx, c_ctx, w_mod, b_mod, g_norm, w_gate_up, w_down, w_in_ab, g_qnorm_a, g_knorm_a,
           sink_b, w_out_ab, w_in_c, g_cq, g_ckv, w_uq, w_ukv, w_out_c, g_final):
    batch, n, d = x.shape
    n_ctx = ctx.shape[1]
    depth = w_mod.shape[0]
    assert batch + 1 <= 8 and n % GRID_W == 0

    cc = jnp.concatenate([c, c_ctx[None], jnp.zeros((8 - batch - 1, d), F32)], axis=0)
    mod4 = _mod_call(cc, w_mod, b_mod)

    w_gu = w_gate_up.astype(BF16)
    w_dn = w_down.astype(BF16)
    w_ab = w_in_ab.astype(BF16)
    w_oab = w_out_ab.astype(BF16)
    w_oc = w_out_c.astype(BF16)

    cos_h, sin_h = _rope_tables(n, HEAD_DIM)
    cos_m, sin_m = _rope_tables(n, MLA_ROPE)
    pad = jnp.zeros((n, 128 - MLA_ROPE), F32)
    tables_h = (cos_h, sin_h)
    tables_m = (jnp.concatenate([cos_m, pad], axis=1), jnp.concatenate([sin_m, pad], axis=1))

    xs = x.reshape(batch * n, d)
    hs = ctx.reshape(batch * n_ctx, d)
    x_mod = dict(mod_base=0, rows_per_mod=n)
    c_mod = dict(mod_base=batch, rows_per_mod=batch * n_ctx)

    for l in range(depth):
        need_ctx = l < depth - 1
        last = l == depth - 1
        i = l // 2
        xs = _ffn_call(xs, mod4, g_norm, w_gu, w_dn, l=l, jf=0, jm=0, **x_mod)
        hs = _ffn_call(hs, mod4, g_norm, w_gu, w_dn, l=l, jf=0, jm=0, **c_mod)
        if l % 2 == 0:
            w_vt = jnp.concatenate([w_ab[i][:, COL_VA * HEAD_DIM:COL_QB * HEAD_DIM],
                                    w_ab[i][:, COL_VB * HEAD_DIM:]], axis=1).T
            qkv_x, vt_x = _proj_ab_call(xs, mod4, g_norm, w_ab[i], w_vt, g_qnorm_a[i], g_knorm_a[i], tables_h,
                                        l=l, n_pos=n, **x_mod)
            qkv_c, vt_c = _proj_ab_call(hs, mod4, g_norm, w_ab[i], w_vt, g_qnorm_a[i], g_knorm_a[i], None,
                                        l=l, n_pos=n, **c_mod)
            oa = _flash_call(qkv_x, (qkv_c, vt_c), (qkv_x, vt_x), None, batch=batch, n_q=n, n_ctx=n_ctx,
                             n_heads=N_HEADS_A, group=GROUP_A, dq=HEAD_DIM, dv=HEAD_DIM,
                             q_col=COL_QA, k_col=COL_KA, v_col=0, tq=min(256, n))
            ob = _window_call(qkv_x, vt_x, qkv_c, vt_c, sink_b[i], batch=batch, n=n, n_ctx=n_ctx)
            w_o = w_oab[i]
            xs_new = _out_call(xs, mod4, [(oa, w_o[:QA]), (ob, w_o[QA:])], l=l, **x_mod)
            if need_ctx:
                oa_c = _flash_call(qkv_c, (qkv_c, vt_c), None, None, batch=batch, n_q=n_ctx, n_ctx=n_ctx,
                                   n_heads=N_HEADS_A, group=GROUP_A, dq=HEAD_DIM, dv=HEAD_DIM,
                                   q_col=COL_QA, k_col=COL_KA, v_col=0, tq=min(128, n_ctx))
                ob_c = _flash_call(qkv_c, (qkv_c, vt_c), None, sink_b[i], batch=batch, n_q=n_ctx, n_ctx=n_ctx,
                                   n_heads=N_HEADS_B, group=GROUP_B, dq=HEAD_DIM, dv=HEAD_DIM,
                                   q_col=COL_QB, k_col=COL_KB, v_col=N_KV_A, tq=min(128, n_ctx))
                hs = _out_call(hs, mod4, [(oa_c, w_o[:QA]), (ob_c, w_o[QA:])], l=l, **c_mod)
            xs = xs_new
        else:
            w_in, wq, wk, wvt = _mla_weight_layout(w_in_c[i], w_uq[i], w_ukv[i])
            q_x, k_x, v_x = _proj_mla_call(xs, mod4, g_norm, w_in, g_cq[i], g_ckv[i], wq, wk, wvt, tables_m,
                                           l=l, n_pos=n, **x_mod)
            q_c, k_c, v_c = _proj_mla_call(hs, mod4, g_norm, w_in, g_cq[i], g_ckv[i], wq, wk, wvt, None,
                                           l=l, n_pos=n, **c_mod)
            o = _flash_call(q_x, (k_c, v_c), (k_x, v_x), None, batch=batch, n_q=n, n_ctx=n_ctx,
                            n_heads=MLA_HEADS, group=1, dq=MLA_QK, dv=MLA_V, q_col=0, k_col=0, v_col=0,
                            tq=min(1024, n))
            xs_new = _out_call(xs, mod4, [(o, w_oc[i])], l=l, **x_mod)
            if need_ctx:
                o_c = _flash_call(q_c, (k_c, v_c), None, None, batch=batch, n_q=n_ctx, n_ctx=n_ctx,
                                  n_heads=MLA_HEADS, group=1, dq=MLA_QK, dv=MLA_V, q_col=0, k_col=0, v_col=0,
                                  tq=n_ctx)
                hs = _out_call(hs, mod4, [(o_c, w_oc[i])], l=l, **c_mod)
            xs = xs_new
        xs = _ffn_call(xs, mod4, g_norm, w_gu, w_dn, l=l, jf=1, jm=2, g_final=g_final if last else None, **x_mod)
        if need_ctx:
            hs = _ffn_call(hs, mod4, g_norm, w_gu, w_dn, l=l, jf=1, jm=2, **c_mod)
    return xs.reshape(batch, n, d)
```

```python
import functools

import jax
import jax.numpy as jnp
import numpy as np
from jax import lax
from jax.experimental import pallas as pl
from jax.experimental.pallas import tpu as pltpu

GRID_W = 64
HEAD_DIM = 128
N_HEADS_A = 8
N_KV_A = 2
N_HEADS_B = 8
N_KV_B = 2
WINDOW = 128
Q_BLOCK = 128
ROPE_THETA = 10000.0
MLA_HEADS = 16
MLA_Q_RANK = 512
MLA_KV_RANK = 256
MLA_NOPE = 128
MLA_ROPE = 64
MLA_V = 128
MLA_QK = 256
N_MOD = 9
EPS = 1e-6
NEG_INF = -1e30
LOG2E = 1.4426950408889634

QA = N_HEADS_A * HEAD_DIM
KA = N_KV_A * HEAD_DIM
QB = N_HEADS_B * HEAD_DIM
KB = N_KV_B * HEAD_DIM
GROUP_A = N_HEADS_A // N_KV_A
GROUP_B = N_HEADS_B // N_KV_B
W_KA = QA
W_VA = W_KA + KA
W_QB = W_VA + KA
W_KB = W_QB + QB
W_VB = W_KB + KB
COL_QA = 0
COL_QB = N_HEADS_A
COL_KA = COL_QB + N_HEADS_B
COL_KB = COL_KA + N_KV_A
N_COLS_QK = COL_KB + N_KV_B
QK_IN = N_COLS_QK * HEAD_DIM
C_IN_PAD = MLA_Q_RANK + MLA_KV_RANK + 128

F32 = jnp.float32
BF16 = jnp.bfloat16
V7X_VMEM_LIMIT_BYTES = 56 * 1024 * 1024
FLASH_UNROLL = 4
WINDOW_QBLOCKS = 2
ADALN_ROWS = 16
ADALN_UNROLL = 8
DOT_COLS = 512
ONES_ROWS = 16
VT_ROWS = HEAD_DIM + ONES_ROWS


def _params(*sem):
    return pltpu.CompilerParams(dimension_semantics=sem, vmem_limit_bytes=V7X_VMEM_LIMIT_BYTES)


def _rms(x):
    return x * lax.rsqrt(jnp.mean(x * x, axis=-1, keepdims=True) + EPS)


def _silu(x):
    return x / (1.0 + jnp.exp(-x))


def _mod_row(mod_ref, k):
    return mod_ref[0, 0, k:k + 1, :]


def _adaln(x, g_ref, jn, mod_ref, j):
    return _rms(x) * g_ref[0, jn:jn + 1, :] * (1.0 + _mod_row(mod_ref, 3 * j + 1)) + _mod_row(mod_ref, 3 * j)


def _adaln_rows(x_ref, out_ref, g_ref, jn, mod_ref, j):
    rows = ADALN_ROWS
    gain = g_ref[0, jn:jn + 1, :] * (1.0 + _mod_row(mod_ref, 3 * j + 1))
    shift = _mod_row(mod_ref, 3 * j)

    def body(r, carry):
        r0 = pl.multiple_of(r * rows, rows)
        x = x_ref[pl.ds(r0, rows), :]
        out_ref[pl.ds(r0, rows), :] = (_rms(x) * gain + shift).astype(BF16)
        return carry

    n_groups = x_ref.shape[0] // rows
    lax.fori_loop(0, n_groups, body, 0, unroll=min(ADALN_UNROLL, n_groups))


def _mod_kernel(c_ref, w_ref, b_ref, o_ref):
    sc = _silu(c_ref[...]).astype(BF16)
    o_ref[0] = jnp.dot(sc, w_ref[0].astype(BF16), preferred_element_type=F32) + b_ref[0]


def _mod_call(cc, w_mod, b_mod):
    depth, d, nd = w_mod.shape
    tn = min(d, 1024)
    out = pl.pallas_call(
        _mod_kernel,
        grid=(depth, nd // tn),
        in_specs=[
            pl.BlockSpec((8, d), lambda l, j: (0, 0)),
            pl.BlockSpec((1, d, tn), lambda l, j: (l, 0, j)),
            pl.BlockSpec((1, 1, tn), lambda l, j: (l, 0, j)),
        ],
        out_specs=pl.BlockSpec((1, 8, tn), lambda l, j: (l, 0, j)),
        out_shape=jax.ShapeDtypeStruct((depth, 8, nd), F32),
        compiler_params=_params("parallel", "parallel"),
        name="mod",
    )(cc, w_mod, b_mod.reshape(depth, 1, nd))
    return out.reshape(depth, 8, N_MOD, d)


def _ffn_kernel(*refs, jm, jn, final):
    if final:
        x_ref, mod_ref, g_ref, wg_ref, wu_ref, wd_ref, gf_ref, o_ref, xn_ref = refs
    else:
        x_ref, mod_ref, g_ref, wg_ref, wu_ref, wd_ref, o_ref, xn_ref = refs
    j = pl.program_id(1)

    @pl.when(j == 0)
    def _():
        _adaln_rows(x_ref, xn_ref, g_ref, jn, mod_ref, jm)
        o_ref[...] = jnp.zeros_like(o_ref)

    xn = xn_ref[...]
    gt = jnp.dot(xn, wg_ref[0, 0], preferred_element_type=F32)
    up = jnp.dot(xn, wu_ref[0, 0], preferred_element_type=F32)
    h = (_silu(gt) * up).astype(BF16)
    o_ref[...] += jnp.dot(h, wd_ref[0, 0], preferred_element_type=F32)

    @pl.when(j == pl.num_programs(1) - 1)
    def _():
        y = x_ref[...] + (0.5 * _mod_row(mod_ref, 3 * jm + 2)) * o_ref[...]
        if final:
            y = _rms(y) * gf_ref[...]
        o_ref[...] = y


def _ffn_call(x, mod4, g_norm, w_gu, w_dn, *, l, jf, jm, mod_base, rows_per_mod, g_final=None):
    t, d = x.shape
    f = w_dn.shape[2]
    tm = min(512, t)
    tf = min(512, f)
    nf = f // tf
    final = g_final is not None
    in_specs = [
        pl.BlockSpec((tm, d), lambda i, j: (i, 0)),
        pl.BlockSpec((1, 1, N_MOD, d), lambda i, j: (l, mod_base + (i * tm) // rows_per_mod, 0, 0)),
        pl.BlockSpec((1, 3, d), lambda i, j: (l, 0, 0)),
        pl.BlockSpec((1, 1, d, tf), lambda i, j: (l, jf, 0, j)),
        pl.BlockSpec((1, 1, d, tf), lambda i, j: (l, jf, 0, nf + j)),
        pl.BlockSpec((1, 1, tf, d), lambda i, j: (l, jf, j, 0)),
    ]
    args = [x, mod4, g_norm, w_gu, w_gu, w_dn]
    if final:
        in_specs.append(pl.BlockSpec((1, d), lambda i, j: (0, 0)))
        args.append(g_final.reshape(1, d))
    return pl.pallas_call(
        functools.partial(_ffn_kernel, jm=jm, jn=jm, final=final),
        grid=(t // tm, nf),
        in_specs=in_specs,
        out_specs=pl.BlockSpec((tm, d), lambda i, j: (i, 0)),
        out_shape=jax.ShapeDtypeStruct((t, d), F32),
        scratch_shapes=[pltpu.VMEM((tm, d), BF16)],
        compiler_params=_params("parallel", "arbitrary"),
        name="ffn",
    )(*args)


def _rope_tables(n_pos, rot_dim, width):
    nf = rot_dim // 4
    pos = np.arange(n_pos)
    inv = np.float32(ROPE_THETA) ** (-np.arange(nf, dtype=np.float32) / np.float32(nf))
    ang_r = (pos // GRID_W).astype(np.float32)[:, None] * inv
    ang_c = (pos % GRID_W).astype(np.float32)[:, None] * inv
    pad = np.zeros((n_pos, width - rot_dim), np.float32)
    cos = np.concatenate([np.cos(ang_r), np.cos(ang_r), np.cos(ang_c), np.cos(ang_c), pad], axis=-1)
    sin = np.concatenate([-np.sin(ang_r), np.sin(ang_r), -np.sin(ang_c), np.sin(ang_c), pad], axis=-1)
    return jnp.asarray(cos, F32), jnp.asarray(sin, F32)


def _rotate(y, cos, sin, nf):
    lanes = y.shape[-1]
    lane = lax.broadcasted_iota(jnp.int32, y.shape, 1)
    partner = jnp.where(lane % (2 * nf) < nf, pltpu.roll(y, lanes - nf, 1), pltpu.roll(y, nf, 1))
    return y * cos + partner * sin


def _store_vt(vt_ref, vt, n_heads):
    ones = jnp.ones((ONES_ROWS, vt.shape[1]), BF16)
    for h in range(n_heads):
        vt_ref[h * VT_ROWS:h * VT_ROWS + HEAD_DIM, :] = vt[h * HEAD_DIM:(h + 1) * HEAD_DIM].astype(BF16)
        vt_ref[h * VT_ROWS + HEAD_DIM:(h + 1) * VT_ROWS, :] = ones


def _proj_ab_kernel(*refs, rope):
    if rope:
        x_ref, mod_ref, g_ref, w_ref, wvt_ref, gq_ref, gk_ref, cos_ref, sin_ref, o_ref, vt_ref = refs
        cos, sin = cos_ref[...], sin_ref[...]
    else:
        x_ref, mod_ref, g_ref, w_ref, wvt_ref, gq_ref, gk_ref, o_ref, vt_ref = refs
    xn = _adaln(x_ref[...], g_ref, 1, mod_ref, 1).astype(BF16)
    scale = HEAD_DIM ** -0.5 * LOG2E
    hpd = DOT_COLS // HEAD_DIM
    for c in range(N_COLS_QK // hpd):
        yc = jnp.dot(xn, w_ref[:, c * DOT_COLS:(c + 1) * DOT_COLS], preferred_element_type=F32)
        for hh in range(hpd):
            col = hpd * c + hh
            y = yc[:, hh * HEAD_DIM:(hh + 1) * HEAD_DIM]
            is_q = col < COL_KA
            if col < COL_QB:
                y = _rms(y) * gq_ref[...]
            if COL_KA <= col < COL_KB:
                y = _rms(y) * gk_ref[...]
            if rope:
                y = _rotate(y, cos, sin, HEAD_DIM // 4)
            if is_q:
                y = y * scale
            o_ref[:, col * HEAD_DIM:(col + 1) * HEAD_DIM] = y.astype(BF16)
    _store_vt(vt_ref, _nt_dot(wvt_ref[...], xn), N_KV_A + N_KV_B)


def _proj_ab_call(x, mod4, g_norm, w_in, w_vt, g_q, g_k, tables, *, l, mod_base, rows_per_mod, n_pos):
    t, d = x.shape
    tm = min(512, t)
    rope = tables is not None
    const = dict(pipeline_mode=pl.Buffered(1))
    in_specs = [
        pl.BlockSpec((tm, d), lambda i: (i, 0)),
        pl.BlockSpec((1, 1, N_MOD, d), lambda i: (l, mod_base + (i * tm) // rows_per_mod, 0, 0)),
        pl.BlockSpec((1, 3, d), lambda i: (l, 0, 0)),
        pl.BlockSpec((d, QK_IN), lambda i: (0, 0), **const),
        pl.BlockSpec((KA + KB, d), lambda i: (0, 0), **const),
        pl.BlockSpec((1, HEAD_DIM), lambda i: (0, 0)),
        pl.BlockSpec((1, HEAD_DIM), lambda i: (0, 0)),
    ]
    args = [x, mod4, g_norm, w_in, w_vt, g_q.reshape(1, HEAD_DIM), g_k.reshape(1, HEAD_DIM)]
    if rope:
        npb = n_pos // tm
        in_specs += [pl.BlockSpec((tm, HEAD_DIM), lambda i: (i % npb, 0))] * 2
        args += list(tables)
    return pl.pallas_call(
        functools.partial(_proj_ab_kernel, rope=rope),
        grid=(t // tm,),
        in_specs=in_specs,
        out_specs=[pl.BlockSpec((tm, QK_IN), lambda i: (i, 0)),
                   pl.BlockSpec(((N_KV_A + N_KV_B) * VT_ROWS, tm), lambda i: (0, i))],
        out_shape=[jax.ShapeDtypeStruct((t, QK_IN), BF16),
                   jax.ShapeDtypeStruct(((N_KV_A + N_KV_B) * VT_ROWS, t), BF16)],
        compiler_params=_params("parallel"),
        name="proj_ab",
    )(*args)


def _proj_mla_kernel(*refs, rope):
    if rope:
        (x_ref, mod_ref, g_ref, wi_ref, gcq_ref, gckv_ref, wuq_ref, wuk_ref, wuvt_ref, cos_ref, sin_ref,
         q_ref, k_ref, vt_ref) = refs
        cos, sin = cos_ref[...], sin_ref[...]
    else:
        x_ref, mod_ref, g_ref, wi_ref, gcq_ref, gckv_ref, wuq_ref, wuk_ref, wuvt_ref, q_ref, k_ref, vt_ref = refs
    xn = _adaln(x_ref[...], g_ref, 1, mod_ref, 1).astype(BF16)
    t = jnp.dot(xn, wi_ref[...], preferred_element_type=F32)
    cq = (_rms(t[:, :MLA_Q_RANK]) * gcq_ref[...]).astype(BF16)
    ckv = (_rms(t[:, MLA_Q_RANK:MLA_Q_RANK + MLA_KV_RANK]) * gckv_ref[...]).astype(BF16)
    kr = t[:, MLA_Q_RANK + MLA_KV_RANK:]
    if rope:
        kr = _rotate(kr, cos, sin, MLA_ROPE // 4)
    kr = kr.astype(BF16)
    scale = (MLA_NOPE + MLA_ROPE) ** -0.5 * LOG2E
    q_heads = DOT_COLS // MLA_QK
    for hp in range(MLA_HEADS // q_heads):
        qc = jnp.dot(cq, wuq_ref[:, hp * DOT_COLS:(hp + 1) * DOT_COLS], preferred_element_type=F32)
        for hh in range(q_heads):
            h = q_heads * hp + hh
            qn = qc[:, hh * MLA_QK:hh * MLA_QK + MLA_NOPE]
            qr = qc[:, hh * MLA_QK + MLA_NOPE:(hh + 1) * MLA_QK]
            if rope:
                qr = _rotate(qr, cos, sin, MLA_ROPE // 4)
            q_ref[:, h * MLA_QK:h * MLA_QK + MLA_NOPE] = (qn * scale).astype(BF16)
            q_ref[:, h * MLA_QK + MLA_NOPE:(h + 1) * MLA_QK] = (qr * scale).astype(BF16)
            k_ref[:, h * MLA_QK + MLA_NOPE:(h + 1) * MLA_QK] = kr
    k_heads = DOT_COLS // MLA_NOPE
    for hp in range(MLA_HEADS // k_heads):
        kn = jnp.dot(ckv, wuk_ref[:, hp * DOT_COLS:(hp + 1) * DOT_COLS], preferred_element_type=F32)
        for hh in range(k_heads):
            h = k_heads * hp + hh
            k_ref[:, h * MLA_QK:h * MLA_QK + MLA_NOPE] = kn[:, hh * MLA_NOPE:(hh + 1) * MLA_NOPE].astype(BF16)
    _store_vt(vt_ref, _nt_dot(wuvt_ref[...], ckv), MLA_HEADS)


def _proj_mla_call(x, mod4, g_norm, w_in, g_cq, g_ckv, w_uq, w_uk, w_uvt, tables, *, l, mod_base, rows_per_mod,
                   n_pos):
    t, d = x.shape
    tm = min(512, t)
    rope = tables is not None
    const = dict(pipeline_mode=pl.Buffered(1))
    in_specs = [
        pl.BlockSpec((tm, d), lambda i: (i, 0)),
        pl.BlockSpec((1, 1, N_MOD, d), lambda i: (l, mod_base + (i * tm) // rows_per_mod, 0, 0)),
        pl.BlockSpec((1, 3, d), lambda i: (l, 0, 0)),
        pl.BlockSpec((d, C_IN_PAD), lambda i: (0, 0), **const),
        pl.BlockSpec((1, MLA_Q_RANK), lambda i: (0, 0)),
        pl.BlockSpec((1, MLA_KV_RANK), lambda i: (0, 0)),
        pl.BlockSpec((MLA_Q_RANK, MLA_HEADS * MLA_QK), lambda i: (0, 0), **const),
        pl.BlockSpec((MLA_KV_RANK, MLA_HEADS * MLA_NOPE), lambda i: (0, 0), **const),
        pl.BlockSpec((MLA_HEADS * MLA_V, MLA_KV_RANK), lambda i: (0, 0), **const),
    ]
    args = [x, mod4, g_norm, w_in, g_cq.reshape(1, -1), g_ckv.reshape(1, -1), w_uq, w_uk, w_uvt]
    if rope:
        npb = n_pos // tm
        in_specs += [pl.BlockSpec((tm, 128), lambda i: (i % npb, 0))] * 2
        args += list(tables)
    return pl.pallas_call(
        functools.partial(_proj_mla_kernel, rope=rope),
        grid=(t // tm,),
        in_specs=in_specs,
        out_specs=[
            pl.BlockSpec((tm, MLA_HEADS * MLA_QK), lambda i: (i, 0)),
            pl.BlockSpec((tm, MLA_HEADS * MLA_QK), lambda i: (i, 0)),
            pl.BlockSpec((MLA_HEADS * VT_ROWS, tm), lambda i: (0, i)),
        ],
        out_shape=[
            jax.ShapeDtypeStruct((t, MLA_HEADS * MLA_QK), BF16),
            jax.ShapeDtypeStruct((t, MLA_HEADS * MLA_QK), BF16),
            jax.ShapeDtypeStruct((MLA_HEADS * VT_ROWS, t), BF16),
        ],
        compiler_params=_params("parallel"),
        name="proj_mla",
    )(*args)


def _nt_dot(a, b):
    return lax.dot_general(a, b, (((1,), (1,)), ((), ())), preferred_element_type=F32)


def _flash_kernel(*refs, group, dq, dv, tk, n_lat, has_sink, unroll):
    refs = list(refs)
    q_ref, kc_ref, vc_ref = refs[:3]
    pos = 3
    if n_lat:
        kl_ref, vl_ref = refs[pos:pos + 2]
        pos += 2
    if has_sink:
        sink_ref = refs[pos]
        pos += 1
    o_ref = refs[pos]
    if n_lat:
        s_ref = refs[pos + 1]
    tq = q_ref.shape[0]
    q = jnp.concatenate([q_ref[:, g * dq:(g + 1) * dq] for g in range(group)], axis=0) if group > 1 else q_ref[...]

    def scores(t, slot):
        s_ref[slot] = _nt_dot(kl_ref[pl.ds(pl.multiple_of(t * tk, tk), tk), :], q)

    s = _nt_dot(kc_ref[...], q)
    if n_lat:
        scores(0, 0)
    m = jnp.max(s, axis=0, keepdims=True)
    if has_sink:
        head0 = pl.program_id(1) * group
        sink = jnp.concatenate(
            [jnp.full((1, tq), sink_ref[head0 + g] * LOG2E, F32) for g in range(group)], axis=1)
        m = jnp.maximum(m, sink)
    p = jnp.exp2(s - m)
    acc = jnp.dot(vc_ref[...], p.astype(BF16), preferred_element_type=F32)

    if n_lat:
        def update(t, slot, m, acc):
            s = s_ref[slot]
            m_new = jnp.maximum(m, jnp.max(s, axis=0, keepdims=True))
            alpha = jnp.exp2(m - m_new)
            p = jnp.exp2(s - m_new)
            r0 = pl.multiple_of(t * tk, tk)
            acc = alpha * acc + jnp.dot(vl_ref[:, pl.ds(r0, tk)], p.astype(BF16), preferred_element_type=F32)
            return m_new, acc

        def pair(tt, carry):
            t0 = 2 * tt
            scores(t0 + 1, 1)
            carry = update(t0, 0, *carry)
            scores(jnp.minimum(t0 + 2, n_lat - 1), 0)
            return update(t0 + 1, 1, *carry)

        m, acc = lax.fori_loop(0, n_lat // 2, pair, (m, acc), unroll=unroll)
        if n_lat % 2:
            m, acc = update(n_lat - 1, 0, m, acc)

    l = acc[dv:dv + 1]
    if has_sink:
        l = l + jnp.exp2(sink - m)
    o = (acc[:dv] / l).T.astype(BF16)
    for g in range(group):
        o_ref[:, g * dv:(g + 1) * dv] = o[g * tq:(g + 1) * tq]


def _flash_call(q_arr, kv_ctx, kv_lat, sink, *, batch, n_q, n_ctx, n_heads, group, dq, dv, q_col, k_col, v_col, tq):
    n_kv = n_heads // group
    nqb = n_q // tq
    has_sink = sink is not None
    in_specs = [
        pl.BlockSpec((tq, group * dq), lambda b, g, i: (b * nqb + i, q_col // group + g)),
        pl.BlockSpec((n_ctx, dq), lambda b, g, i: (b, k_col + g)),
        pl.BlockSpec((dv + ONES_ROWS, n_ctx), lambda b, g, i: (v_col + g, b)),
    ]
    args = [q_arr, kv_ctx[0], kv_ctx[1]]
    n_lat = 0
    tk = 0
    if kv_lat is not None:
        n_rows = kv_lat[0].shape[0] // batch
        tk = min(512, n_rows)
        n_lat = n_rows // tk
        in_specs += [
            pl.BlockSpec((n_rows, dq), lambda b, g, i: (b, k_col + g)),
            pl.BlockSpec((dv + ONES_ROWS, n_rows), lambda b, g, i: (v_col + g, b)),
        ]
        args += [kv_lat[0], kv_lat[1]]
    if has_sink:
        in_specs.append(pl.BlockSpec(memory_space=pltpu.SMEM))
        args.append(sink)
    return pl.pallas_call(
        functools.partial(_flash_kernel, group=group, dq=dq, dv=dv, tk=tk, n_lat=n_lat, has_sink=has_sink,
                          unroll=max(1, min(FLASH_UNROLL, n_lat // 2))),
        grid=(batch, n_kv, nqb),
        scratch_shapes=[pltpu.VMEM((2, tk, group * tq), F32)] if n_lat else [],
        in_specs=in_specs,
        out_specs=pl.BlockSpec((tq, group * dv), lambda b, g, i: (b * nqb + i, g)),
        out_shape=jax.ShapeDtypeStruct((batch * n_q, n_heads * dv), BF16),
        compiler_params=_params("parallel", "parallel", "arbitrary"),
        name="flash",
    )(*args)


def _window_kernel(q_ref, kp_ref, kc_ref, kn_ref, vp_ref, vc_ref, vn_ref, kx_ref, vx_ref, sink_ref, o_ref, *, nb):
    i = pl.program_id(2)
    last = pl.num_programs(2) - 1
    tq = nb * Q_BLOCK
    q = jnp.concatenate([q_ref[:, g * HEAD_DIM:(g + 1) * HEAD_DIM] for g in range(GROUP_B)], axis=0)
    cols = GROUP_B * tq

    def qpos(rows):
        return lax.broadcasted_iota(jnp.int32, (rows, cols), 1) % tq

    def kpos(rows):
        return lax.broadcasted_iota(jnp.int32, (rows, cols), 0)

    off_p = jnp.where(i > 0, 0, tq + Q_BLOCK)
    off_n = jnp.where(i < last, 0, tq + Q_BLOCK)
    r_e, c_e = qpos(Q_BLOCK), kpos(Q_BLOCK)
    s_p = jnp.where(r_e + off_p <= c_e, _nt_dot(kp_ref[...], q), NEG_INF)
    s_n = jnp.where(r_e >= c_e + (tq - WINDOW) + off_n, _nt_dot(kn_ref[...], q), NEG_INF)
    s_c = jnp.where(jnp.abs(qpos(tq) - kpos(tq)) <= WINDOW, _nt_dot(kc_ref[...], q), NEG_INF)
    s_x = _nt_dot(kx_ref[...], q)
    head0 = pl.program_id(1) * GROUP_B
    sink = jnp.concatenate([jnp.full((1, tq), sink_ref[head0 + g] * LOG2E, F32) for g in range(GROUP_B)], axis=1)
    m = jnp.maximum(jnp.maximum(jnp.max(s_p, 0, keepdims=True), jnp.max(s_c, 0, keepdims=True)),
                    jnp.maximum(jnp.max(s_n, 0, keepdims=True), jnp.max(s_x, 0, keepdims=True)))
    m = jnp.maximum(m, sink)
    p_p, p_c, p_n, p_x = jnp.exp2(s_p - m), jnp.exp2(s_c - m), jnp.exp2(s_n - m), jnp.exp2(s_x - m)
    acc = (jnp.dot(vp_ref[...], p_p.astype(BF16), preferred_element_type=F32)
           + jnp.dot(vc_ref[...], p_c.astype(BF16), preferred_element_type=F32)
           + jnp.dot(vn_ref[...], p_n.astype(BF16), preferred_element_type=F32)
           + jnp.dot(vx_ref[...], p_x.astype(BF16), preferred_element_type=F32))
    l = acc[HEAD_DIM:HEAD_DIM + 1] + jnp.exp2(sink - m)
    o = (acc[:HEAD_DIM] / l).T.astype(BF16)
    for g in range(GROUP_B):
        o_ref[:, g * HEAD_DIM:(g + 1) * HEAD_DIM] = o[g * tq:(g + 1) * tq]


def _window_call(qkv_x, vt_x, qkv_c, vt_c, sink, *, batch, n, n_ctx):
    assert WINDOW == Q_BLOCK
    nb = WINDOW_QBLOCKS if n % (WINDOW_QBLOCKS * Q_BLOCK) == 0 else 1
    tq = nb * Q_BLOCK
    steps = n // tq
    nblk = n // Q_BLOCK

    def edge_block(i, shift):
        return jnp.clip(i * nb + shift, 0, nblk - 1)

    in_specs = [
        pl.BlockSpec((tq, GROUP_B * HEAD_DIM), lambda b, g, i: (b * steps + i, COL_QB // GROUP_B + g)),
        pl.BlockSpec((Q_BLOCK, HEAD_DIM), lambda b, g, i: (b * nblk + edge_block(i, -1), COL_KB + g)),
        pl.BlockSpec((tq, HEAD_DIM), lambda b, g, i: (b * steps + i, COL_KB + g)),
        pl.BlockSpec((Q_BLOCK, HEAD_DIM), lambda b, g, i: (b * nblk + edge_block(i, nb), COL_KB + g)),
        pl.BlockSpec((VT_ROWS, Q_BLOCK), lambda b, g, i: (N_KV_A + g, b * nblk + edge_block(i, -1))),
        pl.BlockSpec((VT_ROWS, tq), lambda b, g, i: (N_KV_A + g, b * steps + i)),
        pl.BlockSpec((VT_ROWS, Q_BLOCK), lambda b, g, i: (N_KV_A + g, b * nblk + edge_block(i, nb))),
        pl.BlockSpec((n_ctx, HEAD_DIM), lambda b, g, i: (b, COL_KB + g)),
        pl.BlockSpec((VT_ROWS, n_ctx), lambda b, g, i: (N_KV_A + g, b)),
        pl.BlockSpec(memory_space=pltpu.SMEM),
    ]
    return pl.pallas_call(
        functools.partial(_window_kernel, nb=nb),
        grid=(batch, N_KV_B, steps),
        in_specs=in_specs,
        out_specs=pl.BlockSpec((tq, GROUP_B * HEAD_DIM), lambda b, g, i: (b * steps + i, g)),
        out_shape=jax.ShapeDtypeStruct((batch * n, QB), BF16),
        compiler_params=_params("parallel", "parallel", "arbitrary"),
        name="window",
    )(qkv_x, qkv_x, qkv_x, qkv_x, vt_x, vt_x, vt_x, qkv_c, vt_c, sink)


def _out_kernel(*refs, n_in):
    x_ref, mod_ref = refs[:2]
    o_ref = refs[2 + 2 * n_in]
    y = None
    for k in range(n_in):
        part = jnp.dot(refs[2 + 2 * k][...], refs[3 + 2 * k][...], preferred_element_type=F32)
        y = part if y is None else y + part
    o_ref[...] = x_ref[...] + _mod_row(mod_ref, 5) * y


def _out_call(x, mod4, pairs, *, l, mod_base, rows_per_mod):
    t, d = x.shape
    tm = min(512, t)
    in_specs = [
        pl.BlockSpec((tm, d), lambda i: (i, 0)),
        pl.BlockSpec((1, 1, N_MOD, d), lambda i: (l, mod_base + (i * tm) // rows_per_mod, 0, 0)),
    ]
    args = [x, mod4]
    for o, w in pairs:
        kk = o.shape[1]
        in_specs += [
            pl.BlockSpec((tm, kk), lambda i: (i, 0)),
            pl.BlockSpec((kk, d), lambda i: (0, 0), pipeline_mode=pl.Buffered(1)),
        ]
        args += [o, w]
    return pl.pallas_call(
        functools.partial(_out_kernel, n_in=len(pairs)),
        grid=(t // tm,),
        in_specs=in_specs,
        out_specs=pl.BlockSpec((tm, d), lambda i: (i, 0)),
        out_shape=jax.ShapeDtypeStruct((t, d), F32),
        compiler_params=_params("parallel"),
        name="out_proj",
    )(*args)


def _mla_weight_layout(w_in_c, w_uq, w_ukv):
    d = w_in_c.shape[0]
    w_in = jnp.concatenate([w_in_c, jnp.zeros((d, C_IN_PAD - w_in_c.shape[1]), w_in_c.dtype)], axis=1)
    wq = w_uq.reshape(MLA_Q_RANK, MLA_HEADS, MLA_NOPE + MLA_ROPE)
    wq = jnp.concatenate([wq, jnp.zeros((MLA_Q_RANK, MLA_HEADS, MLA_QK - MLA_NOPE - MLA_ROPE), w_uq.dtype)], axis=-1)
    wkv = w_ukv.reshape(MLA_KV_RANK, MLA_HEADS, MLA_NOPE + MLA_V)
    wk = wkv[:, :, :MLA_NOPE].reshape(MLA_KV_RANK, MLA_HEADS * MLA_NOPE)
    wvt = wkv[:, :, MLA_NOPE:].reshape(MLA_KV_RANK, MLA_HEADS * MLA_V).T
    return (w_in.astype(BF16), wq.reshape(MLA_Q_RANK, MLA_HEADS * MLA_QK).astype(BF16), wk.astype(BF16),
            wvt.astype(BF16))


def kernel(x, c, ctx, c_ctx, w_mod, b_mod, g_norm, w_gate_up, w_down, w_in_ab, g_qnorm_a, g_knorm_a,
           sink_b, w_out_ab, w_in_c, g_cq, g_ckv, w_uq, w_ukv, w_out_c, g_final):
    batch, n, d = x.shape
    n_ctx = ctx.shape[1]
    depth = w_mod.shape[0]
    assert batch + 1 <= 8 and n % GRID_W == 0

    cc = jnp.concatenate([c, c_ctx[None], jnp.zeros((8 - batch - 1, d), F32)], axis=0)
    mod4 = _mod_call(cc, w_mod, b_mod)

    w_gu = w_gate_up.astype(BF16)
    w_dn = w_down.astype(BF16)
    w_ab = w_in_ab.astype(BF16)
    w_oab = w_out_ab.astype(BF16)
    w_oc = w_out_c.astype(BF16)

    tables_h = _rope_tables(n, HEAD_DIM, HEAD_DIM)
    tables_m = _rope_tables(n, MLA_ROPE, 128)

    xs = x.reshape(batch * n, d)
    hs = ctx.reshape(batch * n_ctx, d)
    x_mod = dict(mod_base=0, rows_per_mod=n)
    c_mod = dict(mod_base=batch, rows_per_mod=batch * n_ctx)

    for l in range(depth):
        need_ctx = l < depth - 1
        last = l == depth - 1
        i = l // 2
        xs = _ffn_call(xs, mod4, g_norm, w_gu, w_dn, l=l, jf=0, jm=0, **x_mod)
        hs = _ffn_call(hs, mod4, g_norm, w_gu, w_dn, l=l, jf=0, jm=0, **c_mod)
        if l % 2 == 0:
            w = w_ab[i]
            w_qk = jnp.concatenate([w[:, :W_KA], w[:, W_QB:W_KB], w[:, W_KA:W_VA], w[:, W_KB:W_VB]], axis=1)
            w_vt = jnp.concatenate([w[:, W_VA:W_QB], w[:, W_VB:]], axis=1).T
            qkv_x, vt_x = _proj_ab_call(xs, mod4, g_norm, w_qk, w_vt, g_qnorm_a[i], g_knorm_a[i], tables_h,
                                        l=l, n_pos=n, **x_mod)
            qkv_c, vt_c = _proj_ab_call(hs, mod4, g_norm, w_qk, w_vt, g_qnorm_a[i], g_knorm_a[i], None,
                                        l=l, n_pos=n, **c_mod)
            oa = _flash_call(qkv_x, (qkv_c, vt_c), (qkv_x, vt_x), None, batch=batch, n_q=n, n_ctx=n_ctx,
                             n_heads=N_HEADS_A, group=GROUP_A, dq=HEAD_DIM, dv=HEAD_DIM,
                             q_col=COL_QA, k_col=COL_KA, v_col=0, tq=min(256, n))
            ob = _window_call(qkv_x, vt_x, qkv_c, vt_c, sink_b[i], batch=batch, n=n, n_ctx=n_ctx)
            w_o = w_oab[i]
            xs_new = _out_call(xs, mod4, [(oa, w_o[:QA]), (ob, w_o[QA:])], l=l, **x_mod)
            if need_ctx:
                oa_c = _flash_call(qkv_c, (qkv_c, vt_c), None, None, batch=batch, n_q=n_ctx, n_ctx=n_ctx,
                                   n_heads=N_HEADS_A, group=GROUP_A, dq=HEAD_DIM, dv=HEAD_DIM,
                                   q_col=COL_QA, k_col=COL_KA, v_col=0, tq=min(128, n_ctx))
                ob_c = _flash_call(qkv_c, (qkv_c, vt_c), None, sink_b[i], batch=batch, n_q=n_ctx, n_ctx=n_ctx,
                                   n_heads=N_HEADS_B, group=GROUP_B, dq=HEAD_DIM, dv=HEAD_DIM,
                                   q_col=COL_QB, k_col=COL_KB, v_col=N_KV_A, tq=min(128, n_ctx))
                hs = _out_call(hs, mod4, [(oa_c, w_o[:QA]), (ob_c, w_o[QA:])], l=l, **c_mod)
            xs = xs_new
        else:
            w_in, wq, wk, wvt = _mla_weight_layout(w_in_c[i], w_uq[i], w_ukv[i])
            q_x, k_x, v_x = _proj_mla_call(xs, mod4, g_norm, w_in, g_cq[i], g_ckv[i], wq, wk, wvt, tables_m,
                                           l=l, n_pos=n, **x_mod)
            q_c, k_c, v_c = _proj_mla_call(hs, mod4, g_norm, w_in, g_cq[i], g_ckv[i], wq, wk, wvt, None,
                                           l=l, n_pos=n, **c_mod)
            o = _flash_call(q_x, (k_c, v_c), (k_x, v_x), None, batch=batch, n_q=n, n_ctx=n_ctx,
                            n_heads=MLA_HEADS, group=1, dq=MLA_QK, dv=MLA_V, q_col=0, k_col=0, v_col=0,
                            tq=min(1024, n))
            xs_new = _out_call(xs, mod4, [(o, w_oc[i])], l=l, **x_mod)
            if need_ctx:
                o_c = _flash_call(q_c, (k_c, v_c), None, None, batch=batch, n_q=n_ctx, n_ctx=n_ctx,
                                  n_heads=MLA_HEADS, group=1, dq=MLA_QK, dv=MLA_V, q_col=0, k_col=0, v_col=0,
                                  tq=n_ctx)
                hs = _out_call(hs, mod4, [(o_c, w_oc[i])], l=l, **c_mod)
            xs = xs_new
        xs = _ffn_call(xs, mod4, g_norm, w_gu, w_dn, l=l, jf=1, jm=2, g_final=g_final if last else None, **x_mod)
        if need_ctx:
            hs = _ffn_call(hs, mod4, g_norm, w_gu, w_dn, l=l, jf=1, jm=2, **c_mod)
    return xs.reshape(batch, n, d)
```

```python
import functools

import jax
import jax.numpy as jnp
import numpy as np
from jax import lax
from jax.experimental import pallas as pl
from jax.experimental.pallas import tpu as pltpu

GRID_W = 64
HEAD_DIM = 128
N_HEADS_A = 8
N_KV_A = 2
N_HEADS_B = 8
N_KV_B = 2
WINDOW = 128
Q_BLOCK = 128
ROPE_THETA = 10000.0
MLA_HEADS = 16
MLA_Q_RANK = 512
MLA_KV_RANK = 256
MLA_NOPE = 128
MLA_ROPE = 64
MLA_V = 128
MLA_QK = 256
N_MOD = 9
EPS = 1e-6
NEG_INF = -1e30
LOG2E = 1.4426950408889634

QA = N_HEADS_A * HEAD_DIM
KA = N_KV_A * HEAD_DIM
QB = N_HEADS_B * HEAD_DIM
KB = N_KV_B * HEAD_DIM
GROUP_A = N_HEADS_A // N_KV_A
GROUP_B = N_HEADS_B // N_KV_B
W_KA = QA
W_VA = W_KA + KA
W_QB = W_VA + KA
W_KB = W_QB + QB
W_VB = W_KB + KB
COL_QA = 0
COL_QB = N_HEADS_A
COL_KA = COL_QB + N_HEADS_B
COL_KB = COL_KA + N_KV_A
N_COLS_QK = COL_KB + N_KV_B
QK_IN = N_COLS_QK * HEAD_DIM
C_IN_PAD = MLA_Q_RANK + MLA_KV_RANK + 128

F32 = jnp.float32
BF16 = jnp.bfloat16
V7X_VMEM_LIMIT_BYTES = 56 * 1024 * 1024
FLASH_UNROLL = 4
WINDOW_QBLOCKS = 2
WINDOW_SUBTILES = 4
ADALN_ROWS = 16
ADALN_UNROLL = 8
DOT_COLS = 512
ONES_ROWS = 16
VT_ROWS = HEAD_DIM + ONES_ROWS


def _params(*sem):
    return pltpu.CompilerParams(dimension_semantics=sem, vmem_limit_bytes=V7X_VMEM_LIMIT_BYTES)


def _rms(x):
    return x * lax.rsqrt(jnp.mean(x * x, axis=-1, keepdims=True) + EPS)


def _silu(x):
    return x / (1.0 + jnp.exp(-x))


def _mod_row(mod_ref, k):
    return mod_ref[0, 0, k:k + 1, :]


def _adaln(x, g_ref, jn, mod_ref, j):
    return _rms(x) * g_ref[0, jn:jn + 1, :] * (1.0 + _mod_row(mod_ref, 3 * j + 1)) + _mod_row(mod_ref, 3 * j)


def _adaln_rows(x_ref, out_ref, g_ref, jn, mod_ref, j):
    rows = ADALN_ROWS
    gain = g_ref[0, jn:jn + 1, :] * (1.0 + _mod_row(mod_ref, 3 * j + 1))
    shift = _mod_row(mod_ref, 3 * j)

    def body(r, carry):
        r0 = pl.multiple_of(r * rows, rows)
        x = x_ref[pl.ds(r0, rows), :]
        out_ref[pl.ds(r0, rows), :] = (_rms(x) * gain + shift).astype(BF16)
        return carry

    n_groups = x_ref.shape[0] // rows
    lax.fori_loop(0, n_groups, body, 0, unroll=min(ADALN_UNROLL, n_groups))


def _mod_kernel(c_ref, w_ref, b_ref, o_ref):
    sc = _silu(c_ref[...]).astype(BF16)
    o_ref[0] = jnp.dot(sc, w_ref[0].astype(BF16), preferred_element_type=F32) + b_ref[0]


def _mod_call(cc, w_mod, b_mod):
    depth, d, nd = w_mod.shape
    tn = min(d, 1024)
    out = pl.pallas_call(
        _mod_kernel,
        grid=(depth, nd // tn),
        in_specs=[
            pl.BlockSpec((8, d), lambda l, j: (0, 0)),
            pl.BlockSpec((1, d, tn), lambda l, j: (l, 0, j)),
            pl.BlockSpec((1, 1, tn), lambda l, j: (l, 0, j)),
        ],
        out_specs=pl.BlockSpec((1, 8, tn), lambda l, j: (l, 0, j)),
        out_shape=jax.ShapeDtypeStruct((depth, 8, nd), F32),
        compiler_params=_params("parallel", "parallel"),
        name="mod",
    )(cc, w_mod, b_mod.reshape(depth, 1, nd))
    return out.reshape(depth, 8, N_MOD, d)


def _ffn_kernel(*refs, jm, jn, final, single_step):
    if final:
        x_ref, mod_ref, g_ref, wg_ref, wu_ref, wd_ref, gf_ref, o_ref, xn_ref = refs
    else:
        x_ref, mod_ref, g_ref, wg_ref, wu_ref, wd_ref, o_ref, xn_ref = refs
    j = pl.program_id(1)
    n_steps = pl.num_programs(1)

    def step(first, last):
        if first:
            _adaln_rows(x_ref, xn_ref, g_ref, jn, mod_ref, jm)
        xn = xn_ref[...]
        gt = jnp.dot(xn, wg_ref[0, 0], preferred_element_type=F32)
        up = jnp.dot(xn, wu_ref[0, 0], preferred_element_type=F32)
        h = (_silu(gt) * up).astype(BF16)
        part = jnp.dot(h, wd_ref[0, 0], preferred_element_type=F32)
        acc = part if first else o_ref[...] + part
        if last:
            acc = x_ref[...] + (0.5 * _mod_row(mod_ref, 3 * jm + 2)) * acc
            if final:
                acc = _rms(acc) * gf_ref[...]
        o_ref[...] = acc

    if single_step:
        step(True, True)
    else:
        pl.when(j == 0)(lambda: step(True, False))
        pl.when((j > 0) & (j < n_steps - 1))(lambda: step(False, False))
        pl.when(j == n_steps - 1)(lambda: step(False, True))


def _ffn_call(x, mod4, g_norm, w_gu, w_dn, *, l, jf, jm, mod_base, rows_per_mod, g_final=None):
    t, d = x.shape
    f = w_dn.shape[2]
    tm = min(512, t)
    tf = min(512, f)
    nf = f // tf
    final = g_final is not None
    in_specs = [
        pl.BlockSpec((tm, d), lambda i, j: (i, 0)),
        pl.BlockSpec((1, 1, N_MOD, d), lambda i, j: (l, mod_base + (i * tm) // rows_per_mod, 0, 0)),
        pl.BlockSpec((1, 3, d), lambda i, j: (l, 0, 0)),
        pl.BlockSpec((1, 1, d, tf), lambda i, j: (l, jf, 0, j)),
        pl.BlockSpec((1, 1, d, tf), lambda i, j: (l, jf, 0, nf + j)),
        pl.BlockSpec((1, 1, tf, d), lambda i, j: (l, jf, j, 0)),
    ]
    args = [x, mod4, g_norm, w_gu, w_gu, w_dn]
    if final:
        in_specs.append(pl.BlockSpec((1, d), lambda i, j: (0, 0)))
        args.append(g_final.reshape(1, d))
    return pl.pallas_call(
        functools.partial(_ffn_kernel, jm=jm, jn=jm, final=final, single_step=nf == 1),
        grid=(t // tm, nf),
        in_specs=in_specs,
        out_specs=pl.BlockSpec((tm, d), lambda i, j: (i, 0)),
        out_shape=jax.ShapeDtypeStruct((t, d), F32),
        scratch_shapes=[pltpu.VMEM((tm, d), BF16)],
        compiler_params=_params("parallel", "arbitrary"),
        name="ffn",
    )(*args)


def _rope_tables(n_pos, rot_dim, width):
    nf = rot_dim // 4
    pos = np.arange(n_pos)
    inv = np.float32(ROPE_THETA) ** (-np.arange(nf, dtype=np.float32) / np.float32(nf))
    ang_r = (pos // GRID_W).astype(np.float32)[:, None] * inv
    ang_c = (pos % GRID_W).astype(np.float32)[:, None] * inv
    pad = np.zeros((n_pos, width - rot_dim), np.float32)
    cos = np.concatenate([np.cos(ang_r), np.cos(ang_r), np.cos(ang_c), np.cos(ang_c), pad], axis=-1)
    sin = np.concatenate([-np.sin(ang_r), np.sin(ang_r), -np.sin(ang_c), np.sin(ang_c), pad], axis=-1)
    return jnp.asarray(cos, F32), jnp.asarray(sin, F32)


def _rotate(y, cos, sin, nf):
    lanes = y.shape[-1]
    lane = lax.broadcasted_iota(jnp.int32, y.shape, 1)
    partner = jnp.where(lane % (2 * nf) < nf, pltpu.roll(y, lanes - nf, 1), pltpu.roll(y, nf, 1))
    return y * cos + partner * sin


def _store_vt(vt_ref, vt, n_heads):
    ones = jnp.ones((ONES_ROWS, vt.shape[1]), BF16)
    for h in range(n_heads):
        vt_ref[h * VT_ROWS:h * VT_ROWS + HEAD_DIM, :] = vt[h * HEAD_DIM:(h + 1) * HEAD_DIM].astype(BF16)
        vt_ref[h * VT_ROWS + HEAD_DIM:(h + 1) * VT_ROWS, :] = ones


def _proj_ab_kernel(*refs, rope):
    if rope:
        x_ref, mod_ref, g_ref, w_ref, wvt_ref, gq_ref, gk_ref, cos_ref, sin_ref, o_ref, vt_ref = refs
        cos, sin = cos_ref[...], sin_ref[...]
    else:
        x_ref, mod_ref, g_ref, w_ref, wvt_ref, gq_ref, gk_ref, o_ref, vt_ref = refs
    xn = _adaln(x_ref[...], g_ref, 1, mod_ref, 1).astype(BF16)
    scale = HEAD_DIM ** -0.5 * LOG2E
    hpd = DOT_COLS // HEAD_DIM
    for c in range(N_COLS_QK // hpd):
        yc = jnp.dot(xn, w_ref[:, c * DOT_COLS:(c + 1) * DOT_COLS], preferred_element_type=F32)
        for hh in range(hpd):
            col = hpd * c + hh
            y = yc[:, hh * HEAD_DIM:(hh + 1) * HEAD_DIM]
            is_q = col < COL_KA
            if col < COL_QB:
                y = _rms(y) * gq_ref[...]
            if COL_KA <= col < COL_KB:
                y = _rms(y) * gk_ref[...]
            if rope:
                y = _rotate(y, cos, sin, HEAD_DIM // 4)
            if is_q:
                y = y * scale
            o_ref[:, col * HEAD_DIM:(col + 1) * HEAD_DIM] = y.astype(BF16)
    _store_vt(vt_ref, _nt_dot(wvt_ref[...], xn), N_KV_A + N_KV_B)


def _proj_ab_call(x, mod4, g_norm, w_in, w_vt, g_q, g_k, tables, *, l, mod_base, rows_per_mod, n_pos):
    t, d = x.shape
    tm = min(512, t)
    rope = tables is not None
    const = dict(pipeline_mode=pl.Buffered(1))
    in_specs = [
        pl.BlockSpec((tm, d), lambda i: (i, 0)),
        pl.BlockSpec((1, 1, N_MOD, d), lambda i: (l, mod_base + (i * tm) // rows_per_mod, 0, 0)),
        pl.BlockSpec((1, 3, d), lambda i: (l, 0, 0)),
        pl.BlockSpec((d, QK_IN), lambda i: (0, 0), **const),
        pl.BlockSpec((KA + KB, d), lambda i: (0, 0), **const),
        pl.BlockSpec((1, HEAD_DIM), lambda i: (0, 0)),
        pl.BlockSpec((1, HEAD_DIM), lambda i: (0, 0)),
    ]
    args = [x, mod4, g_norm, w_in, w_vt, g_q.reshape(1, HEAD_DIM), g_k.reshape(1, HEAD_DIM)]
    if rope:
        npb = n_pos // tm
        in_specs += [pl.BlockSpec((tm, HEAD_DIM), lambda i: (i % npb, 0))] * 2
        args += list(tables)
    return pl.pallas_call(
        functools.partial(_proj_ab_kernel, rope=rope),
        grid=(t // tm,),
        in_specs=in_specs,
        out_specs=[pl.BlockSpec((tm, QK_IN), lambda i: (i, 0)),
                   pl.BlockSpec(((N_KV_A + N_KV_B) * VT_ROWS, tm), lambda i: (0, i))],
        out_shape=[jax.ShapeDtypeStruct((t, QK_IN), BF16),
                   jax.ShapeDtypeStruct(((N_KV_A + N_KV_B) * VT_ROWS, t), BF16)],
        compiler_params=_params("parallel"),
        name="proj_ab",
    )(*args)


def _proj_mla_kernel(*refs, rope):
    if rope:
        (x_ref, mod_ref, g_ref, wi_ref, gcq_ref, gckv_ref, wuq_ref, wuk_ref, wuvt_ref, cos_ref, sin_ref,
         q_ref, k_ref, vt_ref) = refs
        cos, sin = cos_ref[...], sin_ref[...]
    else:
        x_ref, mod_ref, g_ref, wi_ref, gcq_ref, gckv_ref, wuq_ref, wuk_ref, wuvt_ref, q_ref, k_ref, vt_ref = refs
    xn = _adaln(x_ref[...], g_ref, 1, mod_ref, 1).astype(BF16)
    t = jnp.dot(xn, wi_ref[...], preferred_element_type=F32)
    cq = (_rms(t[:, :MLA_Q_RANK]) * gcq_ref[...]).astype(BF16)
    ckv = (_rms(t[:, MLA_Q_RANK:MLA_Q_RANK + MLA_KV_RANK]) * gckv_ref[...]).astype(BF16)
    kr = t[:, MLA_Q_RANK + MLA_KV_RANK:]
    if rope:
        kr = _rotate(kr, cos, sin, MLA_ROPE // 4)
    kr = kr.astype(BF16)
    scale = (MLA_NOPE + MLA_ROPE) ** -0.5 * LOG2E
    q_heads = DOT_COLS // MLA_QK
    for hp in range(MLA_HEADS // q_heads):
        qc = jnp.dot(cq, wuq_ref[:, hp * DOT_COLS:(hp + 1) * DOT_COLS], preferred_element_type=F32)
        for hh in range(q_heads):
            h = q_heads * hp + hh
            qn = qc[:, hh * MLA_QK:hh * MLA_QK + MLA_NOPE]
            qr = qc[:, hh * MLA_QK + MLA_NOPE:(hh + 1) * MLA_QK]
            if rope:
                qr = _rotate(qr, cos, sin, MLA_ROPE // 4)
            q_ref[:, h * MLA_QK:h * MLA_QK + MLA_NOPE] = (qn * scale).astype(BF16)
            q_ref[:, h * MLA_QK + MLA_NOPE:(h + 1) * MLA_QK] = (qr * scale).astype(BF16)
            k_ref[:, h * MLA_QK + MLA_NOPE:(h + 1) * MLA_QK] = kr
    k_heads = DOT_COLS // MLA_NOPE
    for hp in range(MLA_HEADS // k_heads):
        kn = jnp.dot(ckv, wuk_ref[:, hp * DOT_COLS:(hp + 1) * DOT_COLS], preferred_element_type=F32)
        for hh in range(k_heads):
            h = k_heads * hp + hh
            k_ref[:, h * MLA_QK:h * MLA_QK + MLA_NOPE] = kn[:, hh * MLA_NOPE:(hh + 1) * MLA_NOPE].astype(BF16)
    _store_vt(vt_ref, _nt_dot(wuvt_ref[...], ckv), MLA_HEADS)


def _proj_mla_call(x, mod4, g_norm, w_in, g_cq, g_ckv, w_uq, w_uk, w_uvt, tables, *, l, mod_base, rows_per_mod,
                   n_pos):
    t, d = x.shape
    tm = min(512, t)
    rope = tables is not None
    const = dict(pipeline_mode=pl.Buffered(1))
    in_specs = [
        pl.BlockSpec((tm, d), lambda i: (i, 0)),
        pl.BlockSpec((1, 1, N_MOD, d), lambda i: (l, mod_base + (i * tm) // rows_per_mod, 0, 0)),
        pl.BlockSpec((1, 3, d), lambda i: (l, 0, 0)),
        pl.BlockSpec((d, C_IN_PAD), lambda i: (0, 0), **const),
        pl.BlockSpec((1, MLA_Q_RANK), lambda i: (0, 0)),
        pl.BlockSpec((1, MLA_KV_RANK), lambda i: (0, 0)),
        pl.BlockSpec((MLA_Q_RANK, MLA_HEADS * MLA_QK), lambda i: (0, 0), **const),
        pl.BlockSpec((MLA_KV_RANK, MLA_HEADS * MLA_NOPE), lambda i: (0, 0), **const),
        pl.BlockSpec((MLA_HEADS * MLA_V, MLA_KV_RANK), lambda i: (0, 0), **const),
    ]
    args = [x, mod4, g_norm, w_in, g_cq.reshape(1, -1), g_ckv.reshape(1, -1), w_uq, w_uk, w_uvt]
    if rope:
        npb = n_pos // tm
        in_specs += [pl.BlockSpec((tm, 128), lambda i: (i % npb, 0))] * 2
        args += list(tables)
    return pl.pallas_call(
        functools.partial(_proj_mla_kernel, rope=rope),
        grid=(t // tm,),
        in_specs=in_specs,
        out_specs=[
            pl.BlockSpec((tm, MLA_HEADS * MLA_QK), lambda i: (i, 0)),
            pl.BlockSpec((tm, MLA_HEADS * MLA_QK), lambda i: (i, 0)),
            pl.BlockSpec((MLA_HEADS * VT_ROWS, tm), lambda i: (0, i)),
        ],
        out_shape=[
            jax.ShapeDtypeStruct((t, MLA_HEADS * MLA_QK), BF16),
            jax.ShapeDtypeStruct((t, MLA_HEADS * MLA_QK), BF16),
            jax.ShapeDtypeStruct((MLA_HEADS * VT_ROWS, t), BF16),
        ],
        compiler_params=_params("parallel"),
        name="proj_mla",
    )(*args)


def _nt_dot(a, b):
    return lax.dot_general(a, b, (((1,), (1,)), ((), ())), preferred_element_type=F32)


def _flash_kernel(*refs, group, dq, dv, tk, n_lat, has_sink, unroll):
    refs = list(refs)
    q_ref, kc_ref, vc_ref = refs[:3]
    pos = 3
    if n_lat:
        kl_ref, vl_ref = refs[pos:pos + 2]
        pos += 2
    if has_sink:
        sink_ref = refs[pos]
        pos += 1
    o_ref = refs[pos]
    if n_lat:
        s_ref = refs[pos + 1]
    tq = q_ref.shape[0]
    q = jnp.concatenate([q_ref[:, g * dq:(g + 1) * dq] for g in range(group)], axis=0) if group > 1 else q_ref[...]

    def scores(t, slot):
        s_ref[slot] = _nt_dot(kl_ref[pl.ds(pl.multiple_of(t * tk, tk), tk), :], q)

    s = _nt_dot(kc_ref[...], q)
    if n_lat:
        scores(0, 0)
    m = jnp.max(s, axis=0, keepdims=True)
    if has_sink:
        head0 = pl.program_id(1) * group
        sink = jnp.concatenate(
            [jnp.full((1, tq), sink_ref[head0 + g] * LOG2E, F32) for g in range(group)], axis=1)
        m = jnp.maximum(m, sink)
    p = jnp.exp2(s - m)
    acc = jnp.dot(vc_ref[...], p.astype(BF16), preferred_element_type=F32)

    if n_lat:
        def update(t, slot, m, acc):
            s = s_ref[slot]
            m_new = jnp.maximum(m, jnp.max(s, axis=0, keepdims=True))
            alpha = jnp.exp2(m - m_new)
            p = jnp.exp2(s - m_new)
            r0 = pl.multiple_of(t * tk, tk)
            acc = alpha * acc + jnp.dot(vl_ref[:, pl.ds(r0, tk)], p.astype(BF16), preferred_element_type=F32)
            return m_new, acc

        def pair(tt, carry):
            t0 = 2 * tt
            scores(t0 + 1, 1)
            carry = update(t0, 0, *carry)
            scores(jnp.minimum(t0 + 2, n_lat - 1), 0)
            return update(t0 + 1, 1, *carry)

        m, acc = lax.fori_loop(0, n_lat // 2, pair, (m, acc), unroll=unroll)
        if n_lat % 2:
            m, acc = update(n_lat - 1, 0, m, acc)

    l = acc[dv:dv + 1]
    if has_sink:
        l = l + jnp.exp2(sink - m)
    o = (acc[:dv] / l).T.astype(BF16)
    for g in range(group):
        o_ref[:, g * dv:(g + 1) * dv] = o[g * tq:(g + 1) * tq]


def _flash_call(q_arr, kv_ctx, kv_lat, sink, *, batch, n_q, n_ctx, n_heads, group, dq, dv, q_col, k_col, v_col, tq):
    n_kv = n_heads // group
    nqb = n_q // tq
    has_sink = sink is not None
    in_specs = [
        pl.BlockSpec((tq, group * dq), lambda b, g, i: (b * nqb + i, q_col // group + g)),
        pl.BlockSpec((n_ctx, dq), lambda b, g, i: (b, k_col + g)),
        pl.BlockSpec((dv + ONES_ROWS, n_ctx), lambda b, g, i: (v_col + g, b)),
    ]
    args = [q_arr, kv_ctx[0], kv_ctx[1]]
    n_lat = 0
    tk = 0
    if kv_lat is not None:
        n_rows = kv_lat[0].shape[0] // batch
        tk = min(512, n_rows)
        n_lat = n_rows // tk
        in_specs += [
            pl.BlockSpec((n_rows, dq), lambda b, g, i: (b, k_col + g)),
            pl.BlockSpec((dv + ONES_ROWS, n_rows), lambda b, g, i: (v_col + g, b)),
        ]
        args += [kv_lat[0], kv_lat[1]]
    if has_sink:
        in_specs.append(pl.BlockSpec(memory_space=pltpu.SMEM))
        args.append(sink)
    return pl.pallas_call(
        functools.partial(_flash_kernel, group=group, dq=dq, dv=dv, tk=tk, n_lat=n_lat, has_sink=has_sink,
                          unroll=max(1, min(FLASH_UNROLL, n_lat // 2))),
        grid=(batch, n_kv, nqb),
        scratch_shapes=[pltpu.VMEM((2, tk, group * tq), F32)] if n_lat else [],
        in_specs=in_specs,
        out_specs=pl.BlockSpec((tq, group * dv), lambda b, g, i: (b * nqb + i, g)),
        out_shape=jax.ShapeDtypeStruct((batch * n_q, n_heads * dv), BF16),
        compiler_params=_params("parallel", "parallel", "arbitrary"),
        name="flash",
    )(*args)


def _window_kernel(q_ref, kp_ref, kc_ref, kn_ref, vp_ref, vc_ref, vn_ref, kx_ref, vx_ref, sink_ref, o_ref, *, nb,
                   n_sub):
    i = pl.program_id(2)
    last = pl.num_programs(2) - 1
    tq = nb * Q_BLOCK
    cols = GROUP_B * tq
    head0 = pl.program_id(1) * GROUP_B
    sink = jnp.concatenate([jnp.full((1, tq), sink_ref[head0 + g] * LOG2E, F32) for g in range(GROUP_B)], axis=1)

    def qpos(rows):
        return lax.broadcasted_iota(jnp.int32, (rows, cols), 1) % tq

    def kpos(rows):
        return lax.broadcasted_iota(jnp.int32, (rows, cols), 0)

    r_e, c_e = qpos(Q_BLOCK), kpos(Q_BLOCK)
    in_window = jnp.abs(qpos(tq) - kpos(tq)) <= WINDOW
    off_p = jnp.where(i > 0, 0, tq + Q_BLOCK)
    off_n = jnp.where(i < last, 0, tq + Q_BLOCK)

    def operands(u):
        lo, hi = u * tq, (u + 1) * tq
        k_prev = kp_ref[...] if u == 0 else kc_ref[lo - Q_BLOCK:lo, :]
        v_prev = vp_ref[...] if u == 0 else vc_ref[:, lo - Q_BLOCK:lo]
        k_next = kn_ref[...] if u == n_sub - 1 else kc_ref[hi:hi + Q_BLOCK, :]
        v_next = vn_ref[...] if u == n_sub - 1 else vc_ref[:, hi:hi + Q_BLOCK]
        return (k_prev, kc_ref[lo:hi, :], k_next), (v_prev, vc_ref[:, lo:hi], v_next)

    def scores(u):
        q = jnp.concatenate([q_ref[u * tq:(u + 1) * tq, g * HEAD_DIM:(g + 1) * HEAD_DIM] for g in range(GROUP_B)],
                            axis=0)
        (k_prev, k_cur, k_next), _ = operands(u)
        s_p = jnp.where(r_e + (off_p if u == 0 else 0) <= c_e, _nt_dot(k_prev, q), NEG_INF)
        s_n = jnp.where(r_e >= c_e + (tq - WINDOW) + (off_n if u == n_sub - 1 else 0), _nt_dot(k_next, q), NEG_INF)
        s_c = jnp.where(in_window, _nt_dot(k_cur, q), NEG_INF)
        return s_p, s_c, s_n, _nt_dot(kx_ref[...], q)

    def attend(u, s):
        s_p, s_c, s_n, s_x = s
        _, (v_prev, v_cur, v_next) = operands(u)
        m = jnp.maximum(jnp.maximum(jnp.max(s_p, 0, keepdims=True), jnp.max(s_c, 0, keepdims=True)),
                        jnp.maximum(jnp.max(s_n, 0, keepdims=True), jnp.max(s_x, 0, keepdims=True)))
        m = jnp.maximum(m, sink)
        p_p, p_c, p_n, p_x = jnp.exp2(s_p - m), jnp.exp2(s_c - m), jnp.exp2(s_n - m), jnp.exp2(s_x - m)
        acc = (jnp.dot(v_prev, p_p.astype(BF16), preferred_element_type=F32)
               + jnp.dot(v_cur, p_c.astype(BF16), preferred_element_type=F32)
               + jnp.dot(v_next, p_n.astype(BF16), preferred_element_type=F32)
               + jnp.dot(vx_ref[...], p_x.astype(BF16), preferred_element_type=F32))
        l = acc[HEAD_DIM:HEAD_DIM + 1] + jnp.exp2(sink - m)
        o = (acc[:HEAD_DIM] / l).T.astype(BF16)
        for g in range(GROUP_B):
            o_ref[u * tq:(u + 1) * tq, g * HEAD_DIM:(g + 1) * HEAD_DIM] = o[g * tq:(g + 1) * tq]

    all_scores = [scores(u) for u in range(n_sub)]
    for u in range(n_sub):
        attend(u, all_scores[u])


def _window_call(qkv_x, vt_x, qkv_c, vt_c, sink, *, batch, n, n_ctx):
    assert WINDOW == Q_BLOCK
    nb = WINDOW_QBLOCKS if n % (WINDOW_QBLOCKS * Q_BLOCK) == 0 else 1
    n_sub = WINDOW_SUBTILES if n % (WINDOW_SUBTILES * nb * Q_BLOCK) == 0 else 1
    sub_tq = nb * Q_BLOCK
    nb = nb * n_sub
    tq = nb * Q_BLOCK
    steps = n // tq
    nblk = n // Q_BLOCK

    def edge_block(i, shift):
        return jnp.clip(i * nb + shift, 0, nblk - 1)

    in_specs = [
        pl.BlockSpec((tq, GROUP_B * HEAD_DIM), lambda b, g, i: (b * steps + i, COL_QB // GROUP_B + g)),
        pl.BlockSpec((Q_BLOCK, HEAD_DIM), lambda b, g, i: (b * nblk + edge_block(i, -1), COL_KB + g)),
        pl.BlockSpec((tq, HEAD_DIM), lambda b, g, i: (b * steps + i, COL_KB + g)),
        pl.BlockSpec((Q_BLOCK, HEAD_DIM), lambda b, g, i: (b * nblk + edge_block(i, nb), COL_KB + g)),
        pl.BlockSpec((VT_ROWS, Q_BLOCK), lambda b, g, i: (N_KV_A + g, b * nblk + edge_block(i, -1))),
        pl.BlockSpec((VT_ROWS, tq), lambda b, g, i: (N_KV_A + g, b * steps + i)),
        pl.BlockSpec((VT_ROWS, Q_BLOCK), lambda b, g, i: (N_KV_A + g, b * nblk + edge_block(i, nb))),
        pl.BlockSpec((n_ctx, HEAD_DIM), lambda b, g, i: (b, COL_KB + g)),
        pl.BlockSpec((VT_ROWS, n_ctx), lambda b, g, i: (N_KV_A + g, b)),
        pl.BlockSpec(memory_space=pltpu.SMEM),
    ]
    return pl.pallas_call(
        functools.partial(_window_kernel, nb=sub_tq // Q_BLOCK, n_sub=n_sub),
        grid=(batch, N_KV_B, steps),
        in_specs=in_specs,
        out_specs=pl.BlockSpec((tq, GROUP_B * HEAD_DIM), lambda b, g, i: (b * steps + i, g)),
        out_shape=jax.ShapeDtypeStruct((batch * n, QB), BF16),
        compiler_params=_params("parallel", "parallel", "arbitrary"),
        name="window",
    )(qkv_x, qkv_x, qkv_x, qkv_x, vt_x, vt_x, vt_x, qkv_c, vt_c, sink)


def _out_kernel(*refs, n_in):
    x_ref, mod_ref = refs[:2]
    o_ref = refs[2 + 2 * n_in]
    y = None
    for k in range(n_in):
        part = jnp.dot(refs[2 + 2 * k][...], refs[3 + 2 * k][...], preferred_element_type=F32)
        y = part if y is None else y + part
    o_ref[...] = x_ref[...] + _mod_row(mod_ref, 5) * y


def _out_call(x, mod4, pairs, *, l, mod_base, rows_per_mod):
    t, d = x.shape
    tm = min(512, t)
    in_specs = [
        pl.BlockSpec((tm, d), lambda i: (i, 0)),
        pl.BlockSpec((1, 1, N_MOD, d), lambda i: (l, mod_base + (i * tm) // rows_per_mod, 0, 0)),
    ]
    args = [x, mod4]
    for o, w in pairs:
        kk = o.shape[1]
        in_specs += [
            pl.BlockSpec((tm, kk), lambda i: (i, 0)),
            pl.BlockSpec((kk, d), lambda i: (0, 0), pipeline_mode=pl.Buffered(1)),
        ]
        args += [o, w]
    return pl.pallas_call(
        functools.partial(_out_kernel, n_in=len(pairs)),
        grid=(t // tm,),
        in_specs=in_specs,
        out_specs=pl.BlockSpec((tm, d), lambda i: (i, 0)),
        out_shape=jax.ShapeDtypeStruct((t, d), F32),
        compiler_params=_params("parallel"),
        name="out_proj",
    )(*args)


def _mla_weight_layout(w_in_c, w_uq, w_ukv):
    d = w_in_c.shape[0]
    w_in = jnp.concatenate([w_in_c, jnp.zeros((d, C_IN_PAD - w_in_c.shape[1]), w_in_c.dtype)], axis=1)
    wq = w_uq.reshape(MLA_Q_RANK, MLA_HEADS, MLA_NOPE + MLA_ROPE)
    wq = jnp.concatenate([wq, jnp.zeros((MLA_Q_RANK, MLA_HEADS, MLA_QK - MLA_NOPE - MLA_ROPE), w_uq.dtype)], axis=-1)
    wkv = w_ukv.reshape(MLA_KV_RANK, MLA_HEADS, MLA_NOPE + MLA_V)
    wk = wkv[:, :, :MLA_NOPE].reshape(MLA_KV_RANK, MLA_HEADS * MLA_NOPE)
    wvt = wkv[:, :, MLA_NOPE:].reshape(MLA_KV_RANK, MLA_HEADS * MLA_V).T
    return (w_in.astype(BF16), wq.reshape(MLA_Q_RANK, MLA_HEADS * MLA_QK).astype(BF16), wk.astype(BF16),
            wvt.astype(BF16))


def kernel(x, c, ctx, c_ctx, w_mod, b_mod, g_norm, w_gate_up, w_down, w_in_ab, g_qnorm_a, g_knorm_a,
           sink_b, w_out_ab, w_in_c, g_cq, g_ckv, w_uq, w_ukv, w_out_c, g_final):
    batch, n, d = x.shape
    n_ctx = ctx.shape[1]
    depth = w_mod.shape[0]
    assert batch + 1 <= 8 and n % GRID_W == 0

    cc = jnp.concatenate([c, c_ctx[None], jnp.zeros((8 - batch - 1, d), F32)], axis=0)
    mod4 = _mod_call(cc, w_mod, b_mod)

    w_gu = w_gate_up.astype(BF16)
    w_dn = w_down.astype(BF16)
    w_ab = w_in_ab.astype(BF16)
    w_oab = w_out_ab.astype(BF16)
    w_oc = w_out_c.astype(BF16)

    tables_h = _rope_tables(n, HEAD_DIM, HEAD_DIM)
    tables_m = _rope_tables(n, MLA_ROPE, 128)

    xs = x.reshape(batch * n, d)
    hs = ctx.reshape(batch * n_ctx, d)
    x_mod = dict(mod_base=0, rows_per_mod=n)
    c_mod = dict(mod_base=batch, rows_per_mod=batch * n_ctx)

    for l in range(depth):
        need_ctx = l < depth - 1
        last = l == depth - 1
        i = l // 2
        xs = _ffn_call(xs, mod4, g_norm, w_gu, w_dn, l=l, jf=0, jm=0, **x_mod)
        hs = _ffn_call(hs, mod4, g_norm, w_gu, w_dn, l=l, jf=0, jm=0, **c_mod)
        if l % 2 == 0:
            w = w_ab[i]
            w_qk = jnp.concatenate([w[:, :W_KA], w[:, W_QB:W_KB], w[:, W_KA:W_VA], w[:, W_KB:W_VB]], axis=1)
            w_vt = jnp.concatenate([w[:, W_VA:W_QB], w[:, W_VB:]], axis=1).T
            qkv_x, vt_x = _proj_ab_call(xs, mod4, g_norm, w_qk, w_vt, g_qnorm_a[i], g_knorm_a[i], tables_h,
                                        l=l, n_pos=n, **x_mod)
            qkv_c, vt_c = _proj_ab_call(hs, mod4, g_norm, w_qk, w_vt, g_qnorm_a[i], g_knorm_a[i], None,
                                        l=l, n_pos=n, **c_mod)
            oa = _flash_call(qkv_x, (qkv_c, vt_c), (qkv_x, vt_x), None, batch=batch, n_q=n, n_ctx=n_ctx,
                             n_heads=N_HEADS_A, group=GROUP_A, dq=HEAD_DIM, dv=HEAD_DIM,
                             q_col=COL_QA, k_col=COL_KA, v_col=0, tq=min(256, n))
            ob = _window_call(qkv_x, vt_x, qkv_c, vt_c, sink_b[i], batch=batch, n=n, n_ctx=n_ctx)
            w_o = w_oab[i]
            xs_new = _out_call(xs, mod4, [(oa, w_o[:QA]), (ob, w_o[QA:])], l=l, **x_mod)
            if need_ctx:
                oa_c = _flash_call(qkv_c, (qkv_c, vt_c), None, None, batch=batch, n_q=n_ctx, n_ctx=n_ctx,
                                   n_heads=N_HEADS_A, group=GROUP_A, dq=HEAD_DIM, dv=HEAD_DIM,
                                   q_col=COL_QA, k_col=COL_KA, v_col=0, tq=min(128, n_ctx))
                ob_c = _flash_call(qkv_c, (qkv_c, vt_c), None, sink_b[i], batch=batch, n_q=n_ctx, n_ctx=n_ctx,
                                   n_heads=N_HEADS_B, group=GROUP_B, dq=HEAD_DIM, dv=HEAD_DIM,
                                   q_col=COL_QB, k_col=COL_KB, v_col=N_KV_A, tq=min(128, n_ctx))
                hs = _out_call(hs, mod4, [(oa_c, w_o[:QA]), (ob_c, w_o[QA:])], l=l, **c_mod)
            xs = xs_new
        else:
            w_in, wq, wk, wvt = _mla_weight_layout(w_in_c[i], w_uq[i], w_ukv[i])
            q_x, k_x, v_x = _proj_mla_call(xs, mod4, g_norm, w_in, g_cq[i], g_ckv[i], wq, wk, wvt, tables_m,
                                           l=l, n_pos=n, **x_mod)
            q_c, k_c, v_c = _proj_mla_call(hs, mod4, g_norm, w_in, g_cq[i], g_ckv[i], wq, wk, wvt, None,
                                           l=l, n_pos=n, **c_mod)
            o = _flash_call(q_x, (k_c, v_c), (k_x, v_x), None, batch=batch, n_q=n, n_ctx=n_ctx,
                            n_heads=MLA_HEADS, group=1, dq=MLA_QK, dv=MLA_V, q_col=0, k_col=0, v_col=0,
                            tq=min(1024, n))
            xs_new = _out_call(xs, mod4, [(o, w_oc[i])], l=l, **x_mod)
            if need_ctx:
                o_c = _flash_call(q_c, (k_c, v_c), None, None, batch=batch, n_q=n_ctx, n_ctx=n_ctx,
                                  n_heads=MLA_HEADS, group=1, dq=MLA_QK, dv=MLA_V, q_col=0, k_col=0, v_col=0,
                                  tq=n_ctx)
                hs = _out_call(hs, mod4, [(o_c, w_oc[i])], l=l, **c_mod)
            xs = xs_new
        xs = _ffn_call(xs, mod4, g_norm, w_gu, w_dn, l=l, jf=1, jm=2, g_final=g_final if last else None, **x_mod)
        if need_ctx:
            hs = _ffn_call(hs, mod4, g_norm, w_gu, w_dn, l=l, jf=1, jm=2, **c_mod)
    return xs.reshape(batch, n, d)
```

```python
import functools

import jax
import jax.numpy as jnp
import numpy as np
from jax import lax
from jax.experimental import pallas as pl
from jax.experimental.pallas import tpu as pltpu

GRID_W = 64
HEAD_DIM = 128
N_HEADS_A = 8
N_KV_A = 2
N_HEADS_B = 8
N_KV_B = 2
WINDOW = 128
Q_BLOCK = 128
ROPE_THETA = 10000.0
MLA_HEADS = 16
MLA_Q_RANK = 512
MLA_KV_RANK = 256
MLA_NOPE = 128
MLA_ROPE = 64
MLA_V = 128
MLA_QK = 256
N_MOD = 9
EPS = 1e-6
NEG_INF = -1e30
LOG2E = 1.4426950408889634

QA = N_HEADS_A * HEAD_DIM
KA = N_KV_A * HEAD_DIM
QB = N_HEADS_B * HEAD_DIM
KB = N_KV_B * HEAD_DIM
GROUP_A = N_HEADS_A // N_KV_A
GROUP_B = N_HEADS_B // N_KV_B
W_KA = QA
W_VA = W_KA + KA
W_QB = W_VA + KA
W_KB = W_QB + QB
W_VB = W_KB + KB
COL_QA = 0
COL_QB = N_HEADS_A
COL_KA = COL_QB + N_HEADS_B
COL_KB = COL_KA + N_KV_A
N_COLS_QK = COL_KB + N_KV_B
QK_IN = N_COLS_QK * HEAD_DIM
C_IN_PAD = MLA_Q_RANK + MLA_KV_RANK + 128

F32 = jnp.float32
BF16 = jnp.bfloat16
V7X_VMEM_LIMIT_BYTES = 56 * 1024 * 1024
FLASH_UNROLL = 4
WINDOW_QBLOCKS = 2
WINDOW_SUBTILES = 4
FFN_FIRST_PARTS = 2
ADALN_ROWS = 16
ADALN_UNROLL = 8
DOT_COLS = 512
ONES_ROWS = 16
VT_ROWS = HEAD_DIM + ONES_ROWS


def _params(*sem):
    return pltpu.CompilerParams(dimension_semantics=sem, vmem_limit_bytes=V7X_VMEM_LIMIT_BYTES)


def _rms(x):
    return x * lax.rsqrt(jnp.mean(x * x, axis=-1, keepdims=True) + EPS)


def _silu(x):
    return x / (1.0 + jnp.exp(-x))


def _mod_row(mod_ref, k):
    return mod_ref[0, 0, k:k + 1, :]


def _adaln(x, g_ref, jn, mod_ref, j):
    return _rms(x) * g_ref[0, jn:jn + 1, :] * (1.0 + _mod_row(mod_ref, 3 * j + 1)) + _mod_row(mod_ref, 3 * j)


def _adaln_rows(x_ref, out_ref, g_ref, jn, mod_ref, j):
    rows = ADALN_ROWS
    gain = g_ref[0, jn:jn + 1, :] * (1.0 + _mod_row(mod_ref, 3 * j + 1))
    shift = _mod_row(mod_ref, 3 * j)

    def body(r, carry):
        r0 = pl.multiple_of(r * rows, rows)
        x = x_ref[pl.ds(r0, rows), :]
        out_ref[pl.ds(r0, rows), :] = (_rms(x) * gain + shift).astype(BF16)
        return carry

    n_groups = x_ref.shape[0] // rows
    lax.fori_loop(0, n_groups, body, 0, unroll=min(ADALN_UNROLL, n_groups))


def _adaln_rows_static(x_ref, out_ref, r0, n_rows, g_ref, jn, mod_ref, j):
    gain = g_ref[0, jn:jn + 1, :] * (1.0 + _mod_row(mod_ref, 3 * j + 1))
    shift = _mod_row(mod_ref, 3 * j)
    for r in range(r0, r0 + n_rows, ADALN_ROWS):
        out_ref[r:r + ADALN_ROWS, :] = (_rms(x_ref[r:r + ADALN_ROWS, :]) * gain + shift).astype(BF16)


def _mod_kernel(c_ref, w_ref, b_ref, o_ref):
    sc = _silu(c_ref[...]).astype(BF16)
    o_ref[0] = jnp.dot(sc, w_ref[0].astype(BF16), preferred_element_type=F32) + b_ref[0]


def _mod_call(cc, w_mod, b_mod):
    depth, d, nd = w_mod.shape
    tn = min(d, 1024)
    out = pl.pallas_call(
        _mod_kernel,
        grid=(depth, nd // tn),
        in_specs=[
            pl.BlockSpec((8, d), lambda l, j: (0, 0)),
            pl.BlockSpec((1, d, tn), lambda l, j: (l, 0, j)),
            pl.BlockSpec((1, 1, tn), lambda l, j: (l, 0, j)),
        ],
        out_specs=pl.BlockSpec((1, 8, tn), lambda l, j: (l, 0, j)),
        out_shape=jax.ShapeDtypeStruct((depth, 8, nd), F32),
        compiler_params=_params("parallel", "parallel"),
        name="mod",
    )(cc, w_mod, b_mod.reshape(depth, 1, nd))
    return out.reshape(depth, 8, N_MOD, d)


def _ffn_kernel(*refs, jm, jn, final, single_step):
    if final:
        x_ref, mod_ref, g_ref, wg_ref, wu_ref, wd_ref, gf_ref, o_ref, xn_ref = refs
    else:
        x_ref, mod_ref, g_ref, wg_ref, wu_ref, wd_ref, o_ref, xn_ref = refs
    j = pl.program_id(1)
    n_steps = pl.num_programs(1)

    tm = x_ref.shape[0]

    def chunk(first, last, r0, rows):
        rs = slice(r0, r0 + rows)
        if first:
            _adaln_rows_static(x_ref, xn_ref, r0, rows, g_ref, jn, mod_ref, jm)
        xn = xn_ref[rs, :]
        gt = jnp.dot(xn, wg_ref[0, 0], preferred_element_type=F32)
        up = jnp.dot(xn, wu_ref[0, 0], preferred_element_type=F32)
        h = (_silu(gt) * up).astype(BF16)
        part = jnp.dot(h, wd_ref[0, 0], preferred_element_type=F32)
        acc = part if first else o_ref[rs, :] + part
        if last:
            acc = x_ref[rs, :] + (0.5 * _mod_row(mod_ref, 3 * jm + 2)) * acc
            if final:
                acc = _rms(acc) * gf_ref[...]
        o_ref[rs, :] = acc

    def step(first, last):
        n_parts = FFN_FIRST_PARTS if first and tm % (FFN_FIRST_PARTS * ADALN_ROWS) == 0 else 1
        for part_idx in range(n_parts):
            chunk(first, last, part_idx * (tm // n_parts), tm // n_parts)

    if single_step:
        step(True, True)
    else:
        pl.when(j == 0)(lambda: step(True, False))
        pl.when((j > 0) & (j < n_steps - 1))(lambda: step(False, False))
        pl.when(j == n_steps - 1)(lambda: step(False, True))


def _ffn_call(x, mod4, g_norm, w_gu, w_dn, *, l, jf, jm, mod_base, rows_per_mod, g_final=None):
    t, d = x.shape
    f = w_dn.shape[2]
    tm = min(512, t)
    tf = min(512, f)
    nf = f // tf
    final = g_final is not None
    in_specs = [
        pl.BlockSpec((tm, d), lambda i, j: (i, 0)),
        pl.BlockSpec((1, 1, N_MOD, d), lambda i, j: (l, mod_base + (i * tm) // rows_per_mod, 0, 0)),
        pl.BlockSpec((1, 3, d), lambda i, j: (l, 0, 0)),
        pl.BlockSpec((1, 1, d, tf), lambda i, j: (l, jf, 0, j)),
        pl.BlockSpec((1, 1, d, tf), lambda i, j: (l, jf, 0, nf + j)),
        pl.BlockSpec((1, 1, tf, d), lambda i, j: (l, jf, j, 0)),
    ]
    args = [x, mod4, g_norm, w_gu, w_gu, w_dn]
    if final:
        in_specs.append(pl.BlockSpec((1, d), lambda i, j: (0, 0)))
        args.append(g_final.reshape(1, d))
    return pl.pallas_call(
        functools.partial(_ffn_kernel, jm=jm, jn=jm, final=final, single_step=nf == 1),
        grid=(t // tm, nf),
        in_specs=in_specs,
        out_specs=pl.BlockSpec((tm, d), lambda i, j: (i, 0)),
        out_shape=jax.ShapeDtypeStruct((t, d), F32),
        scratch_shapes=[pltpu.VMEM((tm, d), BF16)],
        compiler_params=_params("parallel", "arbitrary"),
        name="ffn",
    )(*args)


def _rope_tables(n_pos, rot_dim, width):
    nf = rot_dim // 4
    pos = np.arange(n_pos)
    inv = np.float32(ROPE_THETA) ** (-np.arange(nf, dtype=np.float32) / np.float32(nf))
    ang_r = (pos // GRID_W).astype(np.float32)[:, None] * inv
    ang_c = (pos % GRID_W).astype(np.float32)[:, None] * inv
    pad = np.zeros((n_pos, width - rot_dim), np.float32)
    cos = np.concatenate([np.cos(ang_r), np.cos(ang_r), np.cos(ang_c), np.cos(ang_c), pad], axis=-1)
    sin = np.concatenate([-np.sin(ang_r), np.sin(ang_r), -np.sin(ang_c), np.sin(ang_c), pad], axis=-1)
    return jnp.asarray(cos, F32), jnp.asarray(sin, F32)


def _rotate(y, cos, sin, nf):
    lanes = y.shape[-1]
    lane = lax.broadcasted_iota(jnp.int32, y.shape, 1)
    partner = jnp.where(lane % (2 * nf) < nf, pltpu.roll(y, lanes - nf, 1), pltpu.roll(y, nf, 1))
    return y * cos + partner * sin


def _store_vt(vt_ref, vt, n_heads):
    ones = jnp.ones((ONES_ROWS, vt.shape[1]), BF16)
    for h in range(n_heads):
        vt_ref[h * VT_ROWS:h * VT_ROWS + HEAD_DIM, :] = vt[h * HEAD_DIM:(h + 1) * HEAD_DIM].astype(BF16)
        vt_ref[h * VT_ROWS + HEAD_DIM:(h + 1) * VT_ROWS, :] = ones


def _proj_ab_kernel(*refs, rope):
    if rope:
        x_ref, mod_ref, g_ref, w_ref, wvt_ref, gq_ref, gk_ref, cos_ref, sin_ref, o_ref, vt_ref = refs
        cos, sin = cos_ref[...], sin_ref[...]
    else:
        x_ref, mod_ref, g_ref, w_ref, wvt_ref, gq_ref, gk_ref, o_ref, vt_ref = refs
    xn = _adaln(x_ref[...], g_ref, 1, mod_ref, 1).astype(BF16)
    scale = HEAD_DIM ** -0.5 * LOG2E
    hpd = DOT_COLS // HEAD_DIM
    for c in range(N_COLS_QK // hpd):
        yc = jnp.dot(xn, w_ref[:, c * DOT_COLS:(c + 1) * DOT_COLS], preferred_element_type=F32)
        for hh in range(hpd):
            col = hpd * c + hh
            y = yc[:, hh * HEAD_DIM:(hh + 1) * HEAD_DIM]
            is_q = col < COL_KA
            if col < COL_QB:
                y = _rms(y) * gq_ref[...]
            if COL_KA <= col < COL_KB:
                y = _rms(y) * gk_ref[...]
            if rope:
                y = _rotate(y, cos, sin, HEAD_DIM // 4)
            if is_q:
                y = y * scale
            o_ref[:, col * HEAD_DIM:(col + 1) * HEAD_DIM] = y.astype(BF16)
    _store_vt(vt_ref, _nt_dot(wvt_ref[...], xn), N_KV_A + N_KV_B)


def _proj_ab_call(x, mod4, g_norm, w_in, w_vt, g_q, g_k, tables, *, l, mod_base, rows_per_mod, n_pos):
    t, d = x.shape
    tm = min(512, t)
    rope = tables is not None
    const = dict(pipeline_mode=pl.Buffered(1))
    in_specs = [
        pl.BlockSpec((tm, d), lambda i: (i, 0)),
        pl.BlockSpec((1, 1, N_MOD, d), lambda i: (l, mod_base + (i * tm) // rows_per_mod, 0, 0)),
        pl.BlockSpec((1, 3, d), lambda i: (l, 0, 0)),
        pl.BlockSpec((d, QK_IN), lambda i: (0, 0), **const),
        pl.BlockSpec((KA + KB, d), lambda i: (0, 0), **const),
        pl.BlockSpec((1, HEAD_DIM), lambda i: (0, 0)),
        pl.BlockSpec((1, HEAD_DIM), lambda i: (0, 0)),
    ]
    args = [x, mod4, g_norm, w_in, w_vt, g_q.reshape(1, HEAD_DIM), g_k.reshape(1, HEAD_DIM)]
    if rope:
        npb = n_pos // tm
        in_specs += [pl.BlockSpec((tm, HEAD_DIM), lambda i: (i % npb, 0))] * 2
        args += list(tables)
    return pl.pallas_call(
        functools.partial(_proj_ab_kernel, rope=rope),
        grid=(t // tm,),
        in_specs=in_specs,
        out_specs=[pl.BlockSpec((tm, QK_IN), lambda i: (i, 0)),
                   pl.BlockSpec(((N_KV_A + N_KV_B) * VT_ROWS, tm), lambda i: (0, i))],
        out_shape=[jax.ShapeDtypeStruct((t, QK_IN), BF16),
                   jax.ShapeDtypeStruct(((N_KV_A + N_KV_B) * VT_ROWS, t), BF16)],
        compiler_params=_params("parallel"),
        name="proj_ab",
    )(*args)


def _proj_mla_kernel(*refs, rope):
    if rope:
        (x_ref, mod_ref, g_ref, wi_ref, gcq_ref, gckv_ref, wuq_ref, wuk_ref, wuvt_ref, cos_ref, sin_ref,
         q_ref, k_ref, vt_ref) = refs
        cos, sin = cos_ref[...], sin_ref[...]
    else:
        x_ref, mod_ref, g_ref, wi_ref, gcq_ref, gckv_ref, wuq_ref, wuk_ref, wuvt_ref, q_ref, k_ref, vt_ref = refs
    xn = _adaln(x_ref[...], g_ref, 1, mod_ref, 1).astype(BF16)
    t = jnp.dot(xn, wi_ref[...], preferred_element_type=F32)
    cq = (_rms(t[:, :MLA_Q_RANK]) * gcq_ref[...]).astype(BF16)
    ckv = (_rms(t[:, MLA_Q_RANK:MLA_Q_RANK + MLA_KV_RANK]) * gckv_ref[...]).astype(BF16)
    kr = t[:, MLA_Q_RANK + MLA_KV_RANK:]
    if rope:
        kr = _rotate(kr, cos, sin, MLA_ROPE // 4)
    kr = kr.astype(BF16)
    scale = (MLA_NOPE + MLA_ROPE) ** -0.5 * LOG2E
    q_heads = DOT_COLS // MLA_QK
    for hp in range(MLA_HEADS // q_heads):
        qc = jnp.dot(cq, wuq_ref[:, hp * DOT_COLS:(hp + 1) * DOT_COLS], preferred_element_type=F32)
        for hh in range(q_heads):
            h = q_heads * hp + hh
            qn = qc[:, hh * MLA_QK:hh * MLA_QK + MLA_NOPE]
            qr = qc[:, hh * MLA_QK + MLA_NOPE:(hh + 1) * MLA_QK]
            if rope:
                qr = _rotate(qr, cos, sin, MLA_ROPE // 4)
            q_ref[:, h * MLA_QK:h * MLA_QK + MLA_NOPE] = (qn * scale).astype(BF16)
            q_ref[:, h * MLA_QK + MLA_NOPE:(h + 1) * MLA_QK] = (qr * scale).astype(BF16)
            k_ref[:, h * MLA_QK + MLA_NOPE:(h + 1) * MLA_QK] = kr
    k_heads = DOT_COLS // MLA_NOPE
    for hp in range(MLA_HEADS // k_heads):
        kn = jnp.dot(ckv, wuk_ref[:, hp * DOT_COLS:(hp + 1) * DOT_COLS], preferred_element_type=F32)
        for hh in range(k_heads):
            h = k_heads * hp + hh
            k_ref[:, h * MLA_QK:h * MLA_QK + MLA_NOPE] = kn[:, hh * MLA_NOPE:(hh + 1) * MLA_NOPE].astype(BF16)
    _store_vt(vt_ref, _nt_dot(wuvt_ref[...], ckv), MLA_HEADS)


def _proj_mla_call(x, mod4, g_norm, w_in, g_cq, g_ckv, w_uq, w_uk, w_uvt, tables, *, l, mod_base, rows_per_mod,
                   n_pos):
    t, d = x.shape
    tm = min(512, t)
    rope = tables is not None
    const = dict(pipeline_mode=pl.Buffered(1))
    in_specs = [
        pl.BlockSpec((tm, d), lambda i: (i, 0)),
        pl.BlockSpec((1, 1, N_MOD, d), lambda i: (l, mod_base + (i * tm) // rows_per_mod, 0, 0)),
        pl.BlockSpec((1, 3, d), lambda i: (l, 0, 0)),
        pl.BlockSpec((d, C_IN_PAD), lambda i: (0, 0), **const),
        pl.BlockSpec((1, MLA_Q_RANK), lambda i: (0, 0)),
        pl.BlockSpec((1, MLA_KV_RANK), lambda i: (0, 0)),
        pl.BlockSpec((MLA_Q_RANK, MLA_HEADS * MLA_QK), lambda i: (0, 0), **const),
        pl.BlockSpec((MLA_KV_RANK, MLA_HEADS * MLA_NOPE), lambda i: (0, 0), **const),
        pl.BlockSpec((MLA_HEADS * MLA_V, MLA_KV_RANK), lambda i: (0, 0), **const),
    ]
    args = [x, mod4, g_norm, w_in, g_cq.reshape(1, -1), g_ckv.reshape(1, -1), w_uq, w_uk, w_uvt]
    if rope:
        npb = n_pos // tm
        in_specs += [pl.BlockSpec((tm, 128), lambda i: (i % npb, 0))] * 2
        args += list(tables)
    return pl.pallas_call(
        functools.partial(_proj_mla_kernel, rope=rope),
        grid=(t // tm,),
        in_specs=in_specs,
        out_specs=[
            pl.BlockSpec((tm, MLA_HEADS * MLA_QK), lambda i: (i, 0)),
            pl.BlockSpec((tm, MLA_HEADS * MLA_QK), lambda i: (i, 0)),
            pl.BlockSpec((MLA_HEADS * VT_ROWS, tm), lambda i: (0, i)),
        ],
        out_shape=[
            jax.ShapeDtypeStruct((t, MLA_HEADS * MLA_QK), BF16),
            jax.ShapeDtypeStruct((t, MLA_HEADS * MLA_QK), BF16),
            jax.ShapeDtypeStruct((MLA_HEADS * VT_ROWS, t), BF16),
        ],
        compiler_params=_params("parallel"),
        name="proj_mla",
    )(*args)


def _nt_dot(a, b):
    return lax.dot_general(a, b, (((1,), (1,)), ((), ())), preferred_element_type=F32)


def _flash_kernel(*refs, group, dq, dv, tk, n_lat, has_sink, unroll):
    refs = list(refs)
    q_ref, kc_ref, vc_ref = refs[:3]
    pos = 3
    if n_lat:
        kl_ref, vl_ref = refs[pos:pos + 2]
        pos += 2
    if has_sink:
        sink_ref = refs[pos]
        pos += 1
    o_ref = refs[pos]
    if n_lat:
        s_ref = refs[pos + 1]
    tq = q_ref.shape[0]
    q = jnp.concatenate([q_ref[:, g * dq:(g + 1) * dq] for g in range(group)], axis=0) if group > 1 else q_ref[...]

    def scores(t, slot):
        s_ref[slot] = _nt_dot(kl_ref[pl.ds(pl.multiple_of(t * tk, tk), tk), :], q)

    s = _nt_dot(kc_ref[...], q)
    if n_lat:
        scores(0, 0)
    m = jnp.max(s, axis=0, keepdims=True)
    if has_sink:
        head0 = pl.program_id(1) * group
        sink = jnp.concatenate(
            [jnp.full((1, tq), sink_ref[head0 + g] * LOG2E, F32) for g in range(group)], axis=1)
        m = jnp.maximum(m, sink)
    p = jnp.exp2(s - m)
    acc = jnp.dot(vc_ref[...], p.astype(BF16), preferred_element_type=F32)

    if n_lat:
        def update(t, slot, m, acc):
            s = s_ref[slot]
            m_new = jnp.maximum(m, jnp.max(s, axis=0, keepdims=True))
            alpha = jnp.exp2(m - m_new)
            p = jnp.exp2(s - m_new)
            r0 = pl.multiple_of(t * tk, tk)
            acc = alpha * acc + jnp.dot(vl_ref[:, pl.ds(r0, tk)], p.astype(BF16), preferred_element_type=F32)
            return m_new, acc

        def pair(tt, carry):
            t0 = 2 * tt
            scores(t0 + 1, 1)
            carry = update(t0, 0, *carry)
            scores(jnp.minimum(t0 + 2, n_lat - 1), 0)
            return update(t0 + 1, 1, *carry)

        m, acc = lax.fori_loop(0, n_lat // 2, pair, (m, acc), unroll=unroll)
        if n_lat % 2:
            m, acc = update(n_lat - 1, 0, m, acc)

    l = acc[dv:dv + 1]
    if has_sink:
        l = l + jnp.exp2(sink - m)
    o = (acc[:dv] / l).T.astype(BF16)
    for g in range(group):
        o_ref[:, g * dv:(g + 1) * dv] = o[g * tq:(g + 1) * tq]


def _flash_call(q_arr, kv_ctx, kv_lat, sink, *, batch, n_q, n_ctx, n_heads, group, dq, dv, q_col, k_col, v_col, tq):
    n_kv = n_heads // group
    nqb = n_q // tq
    has_sink = sink is not None
    in_specs = [
        pl.BlockSpec((tq, group * dq), lambda b, g, i: (b * nqb + i, q_col // group + g)),
        pl.BlockSpec((n_ctx, dq), lambda b, g, i: (b, k_col + g)),
        pl.BlockSpec((dv + ONES_ROWS, n_ctx), lambda b, g, i: (v_col + g, b)),
    ]
    args = [q_arr, kv_ctx[0], kv_ctx[1]]
    n_lat = 0
    tk = 0
    if kv_lat is not None:
        n_rows = kv_lat[0].shape[0] // batch
        tk = min(512, n_rows)
        n_lat = n_rows // tk
        in_specs += [
            pl.BlockSpec((n_rows, dq), lambda b, g, i: (b, k_col + g)),
            pl.BlockSpec((dv + ONES_ROWS, n_rows), lambda b, g, i: (v_col + g, b)),
        ]
        args += [kv_lat[0], kv_lat[1]]
    if has_sink:
        in_specs.append(pl.BlockSpec(memory_space=pltpu.SMEM))
        args.append(sink)
    return pl.pallas_call(
        functools.partial(_flash_kernel, group=group, dq=dq, dv=dv, tk=tk, n_lat=n_lat, has_sink=has_sink,
                          unroll=max(1, min(FLASH_UNROLL, n_lat // 2))),
        grid=(batch, n_kv, nqb),
        scratch_shapes=[pltpu.VMEM((2, tk, group * tq), F32)] if n_lat else [],
        in_specs=in_specs,
        out_specs=pl.BlockSpec((tq, group * dv), lambda b, g, i: (b * nqb + i, g)),
        out_shape=jax.ShapeDtypeStruct((batch * n_q, n_heads * dv), BF16),
        compiler_params=_params("parallel", "parallel", "arbitrary"),
        name="flash",
    )(*args)


def _window_kernel(q_ref, kp_ref, kc_ref, kn_ref, vp_ref, vc_ref, vn_ref, kx_ref, vx_ref, sink_ref, o_ref, *, nb,
                   n_sub):
    i = pl.program_id(2)
    last = pl.num_programs(2) - 1
    tq = nb * Q_BLOCK
    cols = GROUP_B * tq
    head0 = pl.program_id(1) * GROUP_B
    sink = jnp.concatenate([jnp.full((1, tq), sink_ref[head0 + g] * LOG2E, F32) for g in range(GROUP_B)], axis=1)

    def qpos(rows):
        return lax.broadcasted_iota(jnp.int32, (rows, cols), 1) % tq

    def kpos(rows):
        return lax.broadcasted_iota(jnp.int32, (rows, cols), 0)

    r_e, c_e = qpos(Q_BLOCK), kpos(Q_BLOCK)
    in_window = jnp.abs(qpos(tq) - kpos(tq)) <= WINDOW
    off_p = jnp.where(i > 0, 0, tq + Q_BLOCK)
    off_n = jnp.where(i < last, 0, tq + Q_BLOCK)

    def operands(u):
        lo, hi = u * tq, (u + 1) * tq
        k_prev = kp_ref[...] if u == 0 else kc_ref[lo - Q_BLOCK:lo, :]
        v_prev = vp_ref[...] if u == 0 else vc_ref[:, lo - Q_BLOCK:lo]
        k_next = kn_ref[...] if u == n_sub - 1 else kc_ref[hi:hi + Q_BLOCK, :]
        v_next = vn_ref[...] if u == n_sub - 1 else vc_ref[:, hi:hi + Q_BLOCK]
        return (k_prev, kc_ref[lo:hi, :], k_next), (v_prev, vc_ref[:, lo:hi], v_next)

    def scores(u):
        q = jnp.concatenate([q_ref[u * tq:(u + 1) * tq, g * HEAD_DIM:(g + 1) * HEAD_DIM] for g in range(GROUP_B)],
                            axis=0)
        (k_prev, k_cur, k_next), _ = operands(u)
        s_p = jnp.where(r_e + (off_p if u == 0 else 0) <= c_e, _nt_dot(k_prev, q), NEG_INF)
        s_n = jnp.where(r_e >= c_e + (tq - WINDOW) + (off_n if u == n_sub - 1 else 0), _nt_dot(k_next, q), NEG_INF)
        s_c = jnp.where(in_window, _nt_dot(k_cur, q), NEG_INF)
        return s_p, s_c, s_n, _nt_dot(kx_ref[...], q)

    def attend(u, s):
        s_p, s_c, s_n, s_x = s
        _, (v_prev, v_cur, v_next) = operands(u)
        m = jnp.maximum(jnp.maximum(jnp.max(s_p, 0, keepdims=True), jnp.max(s_c, 0, keepdims=True)),
                        jnp.maximum(jnp.max(s_n, 0, keepdims=True), jnp.max(s_x, 0, keepdims=True)))
        m = jnp.maximum(m, sink)
        p_p, p_c, p_n, p_x = jnp.exp2(s_p - m), jnp.exp2(s_c - m), jnp.exp2(s_n - m), jnp.exp2(s_x - m)
        acc = (jnp.dot(v_prev, p_p.astype(BF16), preferred_element_type=F32)
               + jnp.dot(v_cur, p_c.astype(BF16), preferred_element_type=F32)
               + jnp.dot(v_next, p_n.astype(BF16), preferred_element_type=F32)
               + jnp.dot(vx_ref[...], p_x.astype(BF16), preferred_element_type=F32))
        l = acc[HEAD_DIM:HEAD_DIM + 1] + jnp.exp2(sink - m)
        o = (acc[:HEAD_DIM] / l).T.astype(BF16)
        for g in range(GROUP_B):
            o_ref[u * tq:(u + 1) * tq, g * HEAD_DIM:(g + 1) * HEAD_DIM] = o[g * tq:(g + 1) * tq]

    all_scores = [scores(u) for u in range(n_sub)]
    for u in range(n_sub):
        attend(u, all_scores[u])


def _window_call(qkv_x, vt_x, qkv_c, vt_c, sink, *, batch, n, n_ctx):
    assert WINDOW == Q_BLOCK
    nb = WINDOW_QBLOCKS if n % (WINDOW_QBLOCKS * Q_BLOCK) == 0 else 1
    n_sub = WINDOW_SUBTILES if n % (WINDOW_SUBTILES * nb * Q_BLOCK) == 0 else 1
    sub_tq = nb * Q_BLOCK
    nb = nb * n_sub
    tq = nb * Q_BLOCK
    steps = n // tq
    nblk = n // Q_BLOCK

    def edge_block(i, shift):
        return jnp.clip(i * nb + shift, 0, nblk - 1)

    in_specs = [
        pl.BlockSpec((tq, GROUP_B * HEAD_DIM), lambda b, g, i: (b * steps + i, COL_QB // GROUP_B + g)),
        pl.BlockSpec((Q_BLOCK, HEAD_DIM), lambda b, g, i: (b * nblk + edge_block(i, -1), COL_KB + g)),
        pl.BlockSpec((tq, HEAD_DIM), lambda b, g, i: (b * steps + i, COL_KB + g)),
        pl.BlockSpec((Q_BLOCK, HEAD_DIM), lambda b, g, i: (b * nblk + edge_block(i, nb), COL_KB + g)),
        pl.BlockSpec((VT_ROWS, Q_BLOCK), lambda b, g, i: (N_KV_A + g, b * nblk + edge_block(i, -1))),
        pl.BlockSpec((VT_ROWS, tq), lambda b, g, i: (N_KV_A + g, b * steps + i)),
        pl.BlockSpec((VT_ROWS, Q_BLOCK), lambda b, g, i: (N_KV_A + g, b * nblk + edge_block(i, nb))),
        pl.BlockSpec((n_ctx, HEAD_DIM), lambda b, g, i: (b, COL_KB + g)),
        pl.BlockSpec((VT_ROWS, n_ctx), lambda b, g, i: (N_KV_A + g, b)),
        pl.BlockSpec(memory_space=pltpu.SMEM),
    ]
    return pl.pallas_call(
        functools.partial(_window_kernel, nb=sub_tq // Q_BLOCK, n_sub=n_sub),
        grid=(batch, N_KV_B, steps),
        in_specs=in_specs,
        out_specs=pl.BlockSpec((tq, GROUP_B * HEAD_DIM), lambda b, g, i: (b * steps + i, g)),
        out_shape=jax.ShapeDtypeStruct((batch * n, QB), BF16),
        compiler_params=_params("parallel", "parallel", "arbitrary"),
        name="window",
    )(qkv_x, qkv_x, qkv_x, qkv_x, vt_x, vt_x, vt_x, qkv_c, vt_c, sink)


def _out_kernel(*refs, n_in):
    x_ref, mod_ref = refs[:2]
    o_ref = refs[2 + 2 * n_in]
    y = None
    for k in range(n_in):
        part = jnp.dot(refs[2 + 2 * k][...], refs[3 + 2 * k][...], preferred_element_type=F32)
        y = part if y is None else y + part
    o_ref[...] = x_ref[...] + _mod_row(mod_ref, 5) * y


def _out_call(x, mod4, pairs, *, l, mod_base, rows_per_mod):
    t, d = x.shape
    tm = min(512, t)
    in_specs = [
        pl.BlockSpec((tm, d), lambda i: (i, 0)),
        pl.BlockSpec((1, 1, N_MOD, d), lambda i: (l, mod_base + (i * tm) // rows_per_mod, 0, 0)),
    ]
    args = [x, mod4]
    for o, w in pairs:
        kk = o.shape[1]
        in_specs += [
            pl.BlockSpec((tm, kk), lambda i: (i, 0)),
            pl.BlockSpec((kk, d), lambda i: (0, 0), pipeline_mode=pl.Buffered(1)),
        ]
        args += [o, w]
    return pl.pallas_call(
        functools.partial(_out_kernel, n_in=len(pairs)),
        grid=(t // tm,),
        in_specs=in_specs,
        out_specs=pl.BlockSpec((tm, d), lambda i: (i, 0)),
        out_shape=jax.ShapeDtypeStruct((t, d), F32),
        compiler_params=_params("parallel"),
        name="out_proj",
    )(*args)


def _mla_weight_layout(w_in_c, w_uq, w_ukv):
    d = w_in_c.shape[0]
    w_in = jnp.concatenate([w_in_c, jnp.zeros((d, C_IN_PAD - w_in_c.shape[1]), w_in_c.dtype)], axis=1)
    wq = w_uq.reshape(MLA_Q_RANK, MLA_HEADS, MLA_NOPE + MLA_ROPE)
    wq = jnp.concatenate([wq, jnp.zeros((MLA_Q_RANK, MLA_HEADS, MLA_QK - MLA_NOPE - MLA_ROPE), w_uq.dtype)], axis=-1)
    wkv = w_ukv.reshape(MLA_KV_RANK, MLA_HEADS, MLA_NOPE + MLA_V)
    wk = wkv[:, :, :MLA_NOPE].reshape(MLA_KV_RANK, MLA_HEADS * MLA_NOPE)
    wvt = wkv[:, :, MLA_NOPE:].reshape(MLA_KV_RANK, MLA_HEADS * MLA_V).T
    return (w_in.astype(BF16), wq.reshape(MLA_Q_RANK, MLA_HEADS * MLA_QK).astype(BF16), wk.astype(BF16),
            wvt.astype(BF16))


def kernel(x, c, ctx, c_ctx, w_mod, b_mod, g_norm, w_gate_up, w_down, w_in_ab, g_qnorm_a, g_knorm_a,
           sink_b, w_out_ab, w_in_c, g_cq, g_ckv, w_uq, w_ukv, w_out_c, g_final):
    batch, n, d = x.shape
    n_ctx = ctx.shape[1]
    depth = w_mod.shape[0]
    assert batch + 1 <= 8 and n % GRID_W == 0

    cc = jnp.concatenate([c, c_ctx[None], jnp.zeros((8 - batch - 1, d), F32)], axis=0)
    mod4 = _mod_call(cc, w_mod, b_mod)

    w_gu = w_gate_up.astype(BF16)
    w_dn = w_down.astype(BF16)
    w_ab = w_in_ab.astype(BF16)
    w_oab = w_out_ab.astype(BF16)
    w_oc = w_out_c.astype(BF16)

    tables_h = _rope_tables(n, HEAD_DIM, HEAD_DIM)
    tables_m = _rope_tables(n, MLA_ROPE, 128)

    xs = x.reshape(batch * n, d)
    hs = ctx.reshape(batch * n_ctx, d)
    x_mod = dict(mod_base=0, rows_per_mod=n)
    c_mod = dict(mod_base=batch, rows_per_mod=batch * n_ctx)

    for l in range(depth):
        need_ctx = l < depth - 1
        last = l == depth - 1
        i = l // 2
        xs = _ffn_call(xs, mod4, g_norm, w_gu, w_dn, l=l, jf=0, jm=0, **x_mod)
        hs = _ffn_call(hs, mod4, g_norm, w_gu, w_dn, l=l, jf=0, jm=0, **c_mod)
        if l % 2 == 0:
            w = w_ab[i]
            w_qk = jnp.concatenate([w[:, :W_KA], w[:, W_QB:W_KB], w[:, W_KA:W_VA], w[:, W_KB:W_VB]], axis=1)
            w_vt = jnp.concatenate([w[:, W_VA:W_QB], w[:, W_VB:]], axis=1).T
            qkv_x, vt_x = _proj_ab_call(xs, mod4, g_norm, w_qk, w_vt, g_qnorm_a[i], g_knorm_a[i], tables_h,
                                        l=l, n_pos=n, **x_mod)
            qkv_c, vt_c = _proj_ab_call(hs, mod4, g_norm, w_qk, w_vt, g_qnorm_a[i], g_knorm_a[i], None,
                                        l=l, n_pos=n, **c_mod)
            oa = _flash_call(qkv_x, (qkv_c, vt_c), (qkv_x, vt_x), None, batch=batch, n_q=n, n_ctx=n_ctx,
                             n_heads=N_HEADS_A, group=GROUP_A, dq=HEAD_DIM, dv=HEAD_DIM,
                             q_col=COL_QA, k_col=COL_KA, v_col=0, tq=min(256, n))
            ob = _window_call(qkv_x, vt_x, qkv_c, vt_c, sink_b[i], batch=batch, n=n, n_ctx=n_ctx)
            w_o = w_oab[i]
            xs_new = _out_call(xs, mod4, [(oa, w_o[:QA]), (ob, w_o[QA:])], l=l, **x_mod)
            if need_ctx:
                oa_c = _flash_call(qkv_c, (qkv_c, vt_c), None, None, batch=batch, n_q=n_ctx, n_ctx=n_ctx,
                                   n_heads=N_HEADS_A, group=GROUP_A, dq=HEAD_DIM, dv=HEAD_DIM,
                                   q_col=COL_QA, k_col=COL_KA, v_col=0, tq=min(128, n_ctx))
                ob_c = _flash_call(qkv_c, (qkv_c, vt_c), None, sink_b[i], batch=batch, n_q=n_ctx, n_ctx=n_ctx,
                                   n_heads=N_HEADS_B, group=GROUP_B, dq=HEAD_DIM, dv=HEAD_DIM,
                                   q_col=COL_QB, k_col=COL_KB, v_col=N_KV_A, tq=min(128, n_ctx))
                hs = _out_call(hs, mod4, [(oa_c, w_o[:QA]), (ob_c, w_o[QA:])], l=l, **c_mod)
            xs = xs_new
        else:
            w_in, wq, wk, wvt = _mla_weight_layout(w_in_c[i], w_uq[i], w_ukv[i])
            q_x, k_x, v_x = _proj_mla_call(xs, mod4, g_norm, w_in, g_cq[i], g_ckv[i], wq, wk, wvt, tables_m,
                                           l=l, n_pos=n, **x_mod)
            q_c, k_c, v_c = _proj_mla_call(hs, mod4, g_norm, w_in, g_cq[i], g_ckv[i], wq, wk, wvt, None,
                                           l=l, n_pos=n, **c_mod)
            o = _flash_call(q_x, (k_c, v_c), (k_x, v_x), None, batch=batch, n_q=n, n_ctx=n_ctx,
                            n_heads=MLA_HEADS, group=1, dq=MLA_QK, dv=MLA_V, q_col=0, k_col=0, v_col=0,
                            tq=min(1024, n))
            xs_new = _out_call(xs, mod4, [(o, w_oc[i])], l=l, **x_mod)
            if need_ctx:
                o_c = _flash_call(q_c, (k_c, v_c), None, None, batch=batch, n_q=n_ctx, n_ctx=n_ctx,
                                  n_heads=MLA_HEADS, group=1, dq=MLA_QK, dv=MLA_V, q_col=0, k_col=0, v_col=0,
                                  tq=n_ctx)
                hs = _out_call(hs, mod4, [(o_c, w_oc[i])], l=l, **c_mod)
            xs = xs_new
        xs = _ffn_call(xs, mod4, g_norm, w_gu, w_dn, l=l, jf=1, jm=2, g_final=g_final if last else None, **x_mod)
        if need_ctx:
            hs = _ffn_call(hs, mod4, g_norm, w_gu, w_dn, l=l, jf=1, jm=2, **c_mod)
    return xs.reshape(batch, n, d)
```

```python
import functools

import jax
import jax.numpy as jnp
import numpy as np
from jax import lax
from jax.experimental import pallas as pl
from jax.experimental.pallas import tpu as pltpu

GRID_W = 64
HEAD_DIM = 128
N_HEADS_A = 8
N_KV_A = 2
N_HEADS_B = 8
N_KV_B = 2
WINDOW = 128
Q_BLOCK = 128
ROPE_THETA = 10000.0
MLA_HEADS = 16
MLA_Q_RANK = 512
MLA_KV_RANK = 256
MLA_NOPE = 128
MLA_ROPE = 64
MLA_V = 128
MLA_QK = 256
N_MOD = 9
EPS = 1e-6
NEG_INF = -1e30
LOG2E = 1.4426950408889634

QA = N_HEADS_A * HEAD_DIM
KA = N_KV_A * HEAD_DIM
QB = N_HEADS_B * HEAD_DIM
KB = N_KV_B * HEAD_DIM
GROUP_A = N_HEADS_A // N_KV_A
GROUP_B = N_HEADS_B // N_KV_B
W_KA = QA
W_VA = W_KA + KA
W_QB = W_VA + KA
W_KB = W_QB + QB
W_VB = W_KB + KB
COL_QA = 0
COL_QB = N_HEADS_A
COL_KA = COL_QB + N_HEADS_B
COL_KB = COL_KA + N_KV_A
N_COLS_QK = COL_KB + N_KV_B
QK_IN = N_COLS_QK * HEAD_DIM
C_IN_PAD = MLA_Q_RANK + MLA_KV_RANK + 128

F32 = jnp.float32
BF16 = jnp.bfloat16
V7X_VMEM_LIMIT_BYTES = 56 * 1024 * 1024
FLASH_UNROLL = 4
WINDOW_QBLOCKS = 2
WINDOW_SUBTILES = 4
FFN_FIRST_PARTS = 2
ADALN_ROWS = 16
ADALN_UNROLL = 8
DOT_COLS = 512
ONES_ROWS = 16
VT_ROWS = HEAD_DIM + ONES_ROWS


def _params(*sem):
    return pltpu.CompilerParams(dimension_semantics=sem, vmem_limit_bytes=V7X_VMEM_LIMIT_BYTES)


def _rms(x):
    return x * lax.rsqrt(jnp.mean(x * x, axis=-1, keepdims=True) + EPS)


def _silu(x):
    return x / (1.0 + jnp.exp(-x))


def _mod_row(mod_ref, k):
    return mod_ref[0, 0, k:k + 1, :]


def _adaln(x, g_ref, jn, mod_ref, j):
    return _rms(x) * g_ref[0, jn:jn + 1, :] * (1.0 + _mod_row(mod_ref, 3 * j + 1)) + _mod_row(mod_ref, 3 * j)


def _adaln_rows(x_ref, out_ref, g_ref, jn, mod_ref, j):
    rows = ADALN_ROWS
    gain = g_ref[0, jn:jn + 1, :] * (1.0 + _mod_row(mod_ref, 3 * j + 1))
    shift = _mod_row(mod_ref, 3 * j)

    def body(r, carry):
        r0 = pl.multiple_of(r * rows, rows)
        x = x_ref[pl.ds(r0, rows), :]
        out_ref[pl.ds(r0, rows), :] = (_rms(x) * gain + shift).astype(BF16)
        return carry

    n_groups = x_ref.shape[0] // rows
    lax.fori_loop(0, n_groups, body, 0, unroll=min(ADALN_UNROLL, n_groups))


def _adaln_rows_static(x_ref, out_ref, r0, n_rows, g_ref, jn, mod_ref, j):
    gain = g_ref[0, jn:jn + 1, :] * (1.0 + _mod_row(mod_ref, 3 * j + 1))
    shift = _mod_row(mod_ref, 3 * j)
    for r in range(r0, r0 + n_rows, ADALN_ROWS):
        out_ref[r:r + ADALN_ROWS, :] = (_rms(x_ref[r:r + ADALN_ROWS, :]) * gain + shift).astype(BF16)


def _mod_kernel(c_ref, w_ref, b_ref, o_ref):
    sc = _silu(c_ref[...]).astype(BF16)
    o_ref[0] = jnp.dot(sc, w_ref[0].astype(BF16), preferred_element_type=F32) + b_ref[0]


def _mod_call(cc, w_mod, b_mod):
    depth, d, nd = w_mod.shape
    tn = min(d, 1024)
    out = pl.pallas_call(
        _mod_kernel,
        grid=(depth, nd // tn),
        in_specs=[
            pl.BlockSpec((8, d), lambda l, j: (0, 0)),
            pl.BlockSpec((1, d, tn), lambda l, j: (l, 0, j)),
            pl.BlockSpec((1, 1, tn), lambda l, j: (l, 0, j)),
        ],
        out_specs=pl.BlockSpec((1, 8, tn), lambda l, j: (l, 0, j)),
        out_shape=jax.ShapeDtypeStruct((depth, 8, nd), F32),
        compiler_params=_params("parallel", "parallel"),
        name="mod",
    )(cc, w_mod, b_mod.reshape(depth, 1, nd))
    return out.reshape(depth, 8, N_MOD, d)


def _ffn_kernel(*refs, jm, jn, final, single_step):
    if final:
        x_ref, mod_ref, g_ref, wg_ref, wu_ref, wd_ref, gf_ref, o_ref, xn_ref = refs
    else:
        x_ref, mod_ref, g_ref, wg_ref, wu_ref, wd_ref, o_ref, xn_ref = refs
    j = pl.program_id(1)
    n_steps = pl.num_programs(1)

    tm = x_ref.shape[0]

    def chunk(first, last, r0, rows):
        rs = slice(r0, r0 + rows)
        if first:
            _adaln_rows_static(x_ref, xn_ref, r0, rows, g_ref, jn, mod_ref, jm)
        xn = xn_ref[rs, :]
        gt = jnp.dot(xn, wg_ref[0, 0], preferred_element_type=F32)
        up = jnp.dot(xn, wu_ref[0, 0], preferred_element_type=F32)
        h = (_silu(gt) * up).astype(BF16)
        part = jnp.dot(h, wd_ref[0, 0], preferred_element_type=F32)
        acc = part if first else o_ref[rs, :] + part
        if last:
            acc = x_ref[rs, :] + (0.5 * _mod_row(mod_ref, 3 * jm + 2)) * acc
            if final:
                acc = _rms(acc) * gf_ref[...]
        o_ref[rs, :] = acc

    def step(first, last):
        n_parts = FFN_FIRST_PARTS if first and tm % (FFN_FIRST_PARTS * ADALN_ROWS) == 0 else 1
        for part_idx in range(n_parts):
            chunk(first, last, part_idx * (tm // n_parts), tm // n_parts)

    if single_step:
        step(True, True)
    else:
        pl.when(j == 0)(lambda: step(True, False))
        pl.when((j > 0) & (j < n_steps - 1))(lambda: step(False, False))
        pl.when(j == n_steps - 1)(lambda: step(False, True))


def _ffn_call(x, mod4, g_norm, w_gu, w_dn, *, l, jf, jm, mod_base, rows_per_mod, g_final=None):
    t, d = x.shape
    f = w_dn.shape[2]
    tm = min(512, t)
    tf = min(512, f)
    nf = f // tf
    final = g_final is not None
    in_specs = [
        pl.BlockSpec((tm, d), lambda i, j: (i, 0)),
        pl.BlockSpec((1, 1, N_MOD, d), lambda i, j: (l, mod_base + (i * tm) // rows_per_mod, 0, 0)),
        pl.BlockSpec((1, 3, d), lambda i, j: (l, 0, 0)),
        pl.BlockSpec((1, 1, d, tf), lambda i, j: (l, jf, 0, j)),
        pl.BlockSpec((1, 1, d, tf), lambda i, j: (l, jf, 0, nf + j)),
        pl.BlockSpec((1, 1, tf, d), lambda i, j: (l, jf, j, 0)),
    ]
    args = [x, mod4, g_norm, w_gu, w_gu, w_dn]
    if final:
        in_specs.append(pl.BlockSpec((1, d), lambda i, j: (0, 0)))
        args.append(g_final.reshape(1, d))
    return pl.pallas_call(
        functools.partial(_ffn_kernel, jm=jm, jn=jm, final=final, single_step=nf == 1),
        grid=(t // tm, nf),
        in_specs=in_specs,
        out_specs=pl.BlockSpec((tm, d), lambda i, j: (i, 0)),
        out_shape=jax.ShapeDtypeStruct((t, d), F32),
        scratch_shapes=[pltpu.VMEM((tm, d), BF16)],
        compiler_params=_params("parallel", "arbitrary"),
        name="ffn",
    )(*args)


def _rope_tables(n_pos, rot_dim, width):
    nf = rot_dim // 4
    pos = np.arange(n_pos)
    inv = np.float32(ROPE_THETA) ** (-np.arange(nf, dtype=np.float32) / np.float32(nf))
    ang_r = (pos // GRID_W).astype(np.float32)[:, None] * inv
    ang_c = (pos % GRID_W).astype(np.float32)[:, None] * inv
    pad = np.zeros((n_pos, width - rot_dim), np.float32)
    cos = np.concatenate([np.cos(ang_r), np.cos(ang_r), np.cos(ang_c), np.cos(ang_c), pad], axis=-1)
    sin = np.concatenate([-np.sin(ang_r), np.sin(ang_r), -np.sin(ang_c), np.sin(ang_c), pad], axis=-1)
    return jnp.asarray(cos, F32), jnp.asarray(sin, F32)


def _rotate(y, cos, sin, nf):
    lanes = y.shape[-1]
    lane = lax.broadcasted_iota(jnp.int32, y.shape, 1)
    partner = jnp.where(lane % (2 * nf) < nf, pltpu.roll(y, lanes - nf, 1), pltpu.roll(y, nf, 1))
    return y * cos + partner * sin


def _store_vt(vt_ref, vt, n_heads):
    ones = jnp.ones((ONES_ROWS, vt.shape[1]), BF16)
    for h in range(n_heads):
        vt_ref[h * VT_ROWS:h * VT_ROWS + HEAD_DIM, :] = vt[h * HEAD_DIM:(h + 1) * HEAD_DIM].astype(BF16)
        vt_ref[h * VT_ROWS + HEAD_DIM:(h + 1) * VT_ROWS, :] = ones


def _proj_ab_kernel(*refs, rope):
    if rope:
        x_ref, mod_ref, g_ref, w_ref, wvt_ref, gq_ref, gk_ref, cos_ref, sin_ref, o_ref, vt_ref = refs
        cos, sin = cos_ref[...], sin_ref[...]
    else:
        x_ref, mod_ref, g_ref, w_ref, wvt_ref, gq_ref, gk_ref, o_ref, vt_ref = refs
    xn = _adaln(x_ref[...], g_ref, 1, mod_ref, 1).astype(BF16)
    scale = HEAD_DIM ** -0.5 * LOG2E
    hpd = DOT_COLS // HEAD_DIM
    for c in range(N_COLS_QK // hpd):
        yc = jnp.dot(xn, w_ref[:, c * DOT_COLS:(c + 1) * DOT_COLS], preferred_element_type=F32)
        for hh in range(hpd):
            col = hpd * c + hh
            y = yc[:, hh * HEAD_DIM:(hh + 1) * HEAD_DIM]
            is_q = col < COL_KA
            if col < COL_QB:
                y = _rms(y) * gq_ref[...]
            if COL_KA <= col < COL_KB:
                y = _rms(y) * gk_ref[...]
            if rope:
                y = _rotate(y, cos, sin, HEAD_DIM // 4)
            if is_q:
                y = y * scale
            o_ref[:, col * HEAD_DIM:(col + 1) * HEAD_DIM] = y.astype(BF16)
    _store_vt(vt_ref, _nt_dot(wvt_ref[...], xn), N_KV_A + N_KV_B)


def _proj_ab_call(x, mod4, g_norm, w_in, w_vt, g_q, g_k, tables, *, l, mod_base, rows_per_mod, n_pos):
    t, d = x.shape
    tm = min(512, t)
    rope = tables is not None
    const = dict(pipeline_mode=pl.Buffered(1))
    in_specs = [
        pl.BlockSpec((tm, d), lambda i: (i, 0)),
        pl.BlockSpec((1, 1, N_MOD, d), lambda i: (l, mod_base + (i * tm) // rows_per_mod, 0, 0)),
        pl.BlockSpec((1, 3, d), lambda i: (l, 0, 0)),
        pl.BlockSpec((d, QK_IN), lambda i: (0, 0), **const),
        pl.BlockSpec((KA + KB, d), lambda i: (0, 0), **const),
        pl.BlockSpec((1, HEAD_DIM), lambda i: (0, 0)),
        pl.BlockSpec((1, HEAD_DIM), lambda i: (0, 0)),
    ]
    args = [x, mod4, g_norm, w_in, w_vt, g_q.reshape(1, HEAD_DIM), g_k.reshape(1, HEAD_DIM)]
    if rope:
        npb = n_pos // tm
        in_specs += [pl.BlockSpec((tm, HEAD_DIM), lambda i: (i % npb, 0))] * 2
        args += list(tables)
    return pl.pallas_call(
        functools.partial(_proj_ab_kernel, rope=rope),
        grid=(t // tm,),
        in_specs=in_specs,
        out_specs=[pl.BlockSpec((tm, QK_IN), lambda i: (i, 0)),
                   pl.BlockSpec(((N_KV_A + N_KV_B) * VT_ROWS, tm), lambda i: (0, i))],
        out_shape=[jax.ShapeDtypeStruct((t, QK_IN), BF16),
                   jax.ShapeDtypeStruct(((N_KV_A + N_KV_B) * VT_ROWS, t), BF16)],
        compiler_params=_params("parallel"),
        name="proj_ab",
    )(*args)


def _proj_mla_kernel(*refs, rope):
    if rope:
        (x_ref, mod_ref, g_ref, wi_ref, gcq_ref, gckv_ref, wuq_ref, wuk_ref, wuvt_ref, cos_ref, sin_ref,
         q_ref, k_ref, vt_ref) = refs
        cos, sin = cos_ref[...], sin_ref[...]
    else:
        x_ref, mod_ref, g_ref, wi_ref, gcq_ref, gckv_ref, wuq_ref, wuk_ref, wuvt_ref, q_ref, k_ref, vt_ref = refs
    xn = _adaln(x_ref[...], g_ref, 1, mod_ref, 1).astype(BF16)
    t = jnp.dot(xn, wi_ref[...], preferred_element_type=F32)
    cq = (_rms(t[:, :MLA_Q_RANK]) * gcq_ref[...]).astype(BF16)
    ckv = (_rms(t[:, MLA_Q_RANK:MLA_Q_RANK + MLA_KV_RANK]) * gckv_ref[...]).astype(BF16)
    kr = t[:, MLA_Q_RANK + MLA_KV_RANK:]
    if rope:
        kr = _rotate(kr, cos, sin, MLA_ROPE // 4)
    kr = kr.astype(BF16)
    scale = (MLA_NOPE + MLA_ROPE) ** -0.5 * LOG2E
    q_heads = DOT_COLS // MLA_QK
    for hp in range(MLA_HEADS // q_heads):
        qc = jnp.dot(cq, wuq_ref[:, hp * DOT_COLS:(hp + 1) * DOT_COLS], preferred_element_type=F32)
        for hh in range(q_heads):
            h = q_heads * hp + hh
            qn = qc[:, hh * MLA_QK:hh * MLA_QK + MLA_NOPE]
            qr = qc[:, hh * MLA_QK + MLA_NOPE:(hh + 1) * MLA_QK]
            if rope:
                qr = _rotate(qr, cos, sin, MLA_ROPE // 4)
            q_ref[:, h * MLA_QK:h * MLA_QK + MLA_NOPE] = (qn * scale).astype(BF16)
            q_ref[:, h * MLA_QK + MLA_NOPE:(h + 1) * MLA_QK] = (qr * scale).astype(BF16)
            k_ref[:, h * MLA_QK + MLA_NOPE:(h + 1) * MLA_QK] = kr
    k_heads = DOT_COLS // MLA_NOPE
    for hp in range(MLA_HEADS // k_heads):
        kn = jnp.dot(ckv, wuk_ref[:, hp * DOT_COLS:(hp + 1) * DOT_COLS], preferred_element_type=F32)
        for hh in range(k_heads):
            h = k_heads * hp + hh
            k_ref[:, h * MLA_QK:h * MLA_QK + MLA_NOPE] = kn[:, hh * MLA_NOPE:(hh + 1) * MLA_NOPE].astype(BF16)
    _store_vt(vt_ref, _nt_dot(wuvt_ref[...], ckv), MLA_HEADS)


def _proj_mla_call(x, mod4, g_norm, w_in, g_cq, g_ckv, w_uq, w_uk, w_uvt, tables, *, l, mod_base, rows_per_mod,
                   n_pos):
    t, d = x.shape
    tm = min(512, t)
    rope = tables is not None
    const = dict(pipeline_mode=pl.Buffered(1))
    in_specs = [
        pl.BlockSpec((tm, d), lambda i: (i, 0)),
        pl.BlockSpec((1, 1, N_MOD, d), lambda i: (l, mod_base + (i * tm) // rows_per_mod, 0, 0)),
        pl.BlockSpec((1, 3, d), lambda i: (l, 0, 0)),
        pl.BlockSpec((d, C_IN_PAD), lambda i: (0, 0), **const),
        pl.BlockSpec((1, MLA_Q_RANK), lambda i: (0, 0)),
        pl.BlockSpec((1, MLA_KV_RANK), lambda i: (0, 0)),
        pl.BlockSpec((MLA_Q_RANK, MLA_HEADS * MLA_QK), lambda i: (0, 0), **const),
        pl.BlockSpec((MLA_KV_RANK, MLA_HEADS * MLA_NOPE), lambda i: (0, 0), **const),
        pl.BlockSpec((MLA_HEADS * MLA_V, MLA_KV_RANK), lambda i: (0, 0), **const),
    ]
    args = [x, mod4, g_norm, w_in, g_cq.reshape(1, -1), g_ckv.reshape(1, -1), w_uq, w_uk, w_uvt]
    if rope:
        npb = n_pos // tm
        in_specs += [pl.BlockSpec((tm, 128), lambda i: (i % npb, 0))] * 2
        args += list(tables)
    return pl.pallas_call(
        functools.partial(_proj_mla_kernel, rope=rope),
        grid=(t // tm,),
        in_specs=in_specs,
        out_specs=[
            pl.BlockSpec((tm, MLA_HEADS * MLA_QK), lambda i: (i, 0)),
            pl.BlockSpec((tm, MLA_HEADS * MLA_QK), lambda i: (i, 0)),
            pl.BlockSpec((MLA_HEADS * VT_ROWS, tm), lambda i: (0, i)),
        ],
        out_shape=[
            jax.ShapeDtypeStruct((t, MLA_HEADS * MLA_QK), BF16),
            jax.ShapeDtypeStruct((t, MLA_HEADS * MLA_QK), BF16),
            jax.ShapeDtypeStruct((MLA_HEADS * VT_ROWS, t), BF16),
        ],
        compiler_params=_params("parallel"),
        name="proj_mla",
    )(*args)


def _nt_dot(a, b):
    return lax.dot_general(a, b, (((1,), (1,)), ((), ())), preferred_element_type=F32)


def _flash_kernel(*refs, group, dq, dv, tk, n_lat, has_sink, unroll):
    refs = list(refs)
    q_ref, kc_ref, vc_ref = refs[:3]
    pos = 3
    if n_lat:
        kl_ref, vl_ref = refs[pos:pos + 2]
        pos += 2
    if has_sink:
        sink_ref = refs[pos]
        pos += 1
    o_ref = refs[pos]
    if n_lat:
        s_ref = refs[pos + 1]
    tq = q_ref.shape[0]
    q = jnp.concatenate([q_ref[:, g * dq:(g + 1) * dq] for g in range(group)], axis=0) if group > 1 else q_ref[...]

    def scores(t, slot):
        s_ref[slot] = _nt_dot(kl_ref[pl.ds(pl.multiple_of(t * tk, tk), tk), :], q)

    s = _nt_dot(kc_ref[...], q)
    if n_lat:
        scores(0, 0)
    m = jnp.max(s, axis=0, keepdims=True)
    if has_sink:
        head0 = pl.program_id(1) * group
        sink = jnp.concatenate(
            [jnp.full((1, tq), sink_ref[head0 + g] * LOG2E, F32) for g in range(group)], axis=1)
        m = jnp.maximum(m, sink)
    p = jnp.exp2(s - m)
    acc = jnp.dot(vc_ref[...], p.astype(BF16), preferred_element_type=F32)

    if n_lat:
        def update(t, slot, m, acc):
            s = s_ref[slot]
            m_new = jnp.maximum(m, jnp.max(s, axis=0, keepdims=True))
            alpha = jnp.exp2(m - m_new)
            p = jnp.exp2(s - m_new)
            r0 = pl.multiple_of(t * tk, tk)
            acc = alpha * acc + jnp.dot(vl_ref[:, pl.ds(r0, tk)], p.astype(BF16), preferred_element_type=F32)
            return m_new, acc

        def pair(tt, carry):
            t0 = 2 * tt
            scores(t0 + 1, 1)
            carry = update(t0, 0, *carry)
            scores(jnp.minimum(t0 + 2, n_lat - 1), 0)
            return update(t0 + 1, 1, *carry)

        m, acc = lax.fori_loop(0, n_lat // 2, pair, (m, acc), unroll=unroll)
        if n_lat % 2:
            m, acc = update(n_lat - 1, 0, m, acc)

    l = acc[dv:dv + 1]
    if has_sink:
        l = l + jnp.exp2(sink - m)
    o = (acc[:dv] / l).astype(BF16)
    for g in range(group):
        o_ref[g * dv:(g + 1) * dv, :] = o[:, g * tq:(g + 1) * tq]


def _flash_call(q_arr, kv_ctx, kv_lat, sink, *, batch, n_q, n_ctx, n_heads, group, dq, dv, q_col, k_col, v_col, tq):
    n_kv = n_heads // group
    nqb = n_q // tq
    has_sink = sink is not None
    in_specs = [
        pl.BlockSpec((tq, group * dq), lambda b, g, i: (b * nqb + i, q_col // group + g)),
        pl.BlockSpec((n_ctx, dq), lambda b, g, i: (b, k_col + g)),
        pl.BlockSpec((dv + ONES_ROWS, n_ctx), lambda b, g, i: (v_col + g, b)),
    ]
    args = [q_arr, kv_ctx[0], kv_ctx[1]]
    n_lat = 0
    tk = 0
    if kv_lat is not None:
        n_rows = kv_lat[0].shape[0] // batch
        tk = min(512, n_rows)
        n_lat = n_rows // tk
        in_specs += [
            pl.BlockSpec((n_rows, dq), lambda b, g, i: (b, k_col + g)),
            pl.BlockSpec((dv + ONES_ROWS, n_rows), lambda b, g, i: (v_col + g, b)),
        ]
        args += [kv_lat[0], kv_lat[1]]
    if has_sink:
        in_specs.append(pl.BlockSpec(memory_space=pltpu.SMEM))
        args.append(sink)
    return pl.pallas_call(
        functools.partial(_flash_kernel, group=group, dq=dq, dv=dv, tk=tk, n_lat=n_lat, has_sink=has_sink,
                          unroll=max(1, min(FLASH_UNROLL, n_lat // 2))),
        grid=(batch, n_kv, nqb),
        scratch_shapes=[pltpu.VMEM((2, tk, group * tq), F32)] if n_lat else [],
        in_specs=in_specs,
        out_specs=pl.BlockSpec((group * dv, tq), lambda b, g, i: (g, b * nqb + i)),
        out_shape=jax.ShapeDtypeStruct((n_heads * dv, batch * n_q), BF16),
        compiler_params=_params("parallel", "parallel", "arbitrary"),
        name="flash",
    )(*args)


def _window_kernel(q_ref, kp_ref, kc_ref, kn_ref, vp_ref, vc_ref, vn_ref, kx_ref, vx_ref, sink_ref, o_ref, *, nb,
                   n_sub):
    i = pl.program_id(2)
    last = pl.num_programs(2) - 1
    tq = nb * Q_BLOCK
    cols = GROUP_B * tq
    head0 = pl.program_id(1) * GROUP_B
    sink = jnp.concatenate([jnp.full((1, tq), sink_ref[head0 + g] * LOG2E, F32) for g in range(GROUP_B)], axis=1)

    def qpos(rows):
        return lax.broadcasted_iota(jnp.int32, (rows, cols), 1) % tq

    def kpos(rows):
        return lax.broadcasted_iota(jnp.int32, (rows, cols), 0)

    r_e, c_e = qpos(Q_BLOCK), kpos(Q_BLOCK)
    in_window = jnp.abs(qpos(tq) - kpos(tq)) <= WINDOW
    off_p = jnp.where(i > 0, 0, tq + Q_BLOCK)
    off_n = jnp.where(i < last, 0, tq + Q_BLOCK)

    def operands(u):
        lo, hi = u * tq, (u + 1) * tq
        k_prev = kp_ref[...] if u == 0 else kc_ref[lo - Q_BLOCK:lo, :]
        v_prev = vp_ref[...] if u == 0 else vc_ref[:, lo - Q_BLOCK:lo]
        k_next = kn_ref[...] if u == n_sub - 1 else kc_ref[hi:hi + Q_BLOCK, :]
        v_next = vn_ref[...] if u == n_sub - 1 else vc_ref[:, hi:hi + Q_BLOCK]
        return (k_prev, kc_ref[lo:hi, :], k_next), (v_prev, vc_ref[:, lo:hi], v_next)

    def scores(u):
        q = jnp.concatenate([q_ref[u * tq:(u + 1) * tq, g * HEAD_DIM:(g + 1) * HEAD_DIM] for g in range(GROUP_B)],
                            axis=0)
        (k_prev, k_cur, k_next), _ = operands(u)
        s_p = jnp.where(r_e + (off_p if u == 0 else 0) <= c_e, _nt_dot(k_prev, q), NEG_INF)
        s_n = jnp.where(r_e >= c_e + (tq - WINDOW) + (off_n if u == n_sub - 1 else 0), _nt_dot(k_next, q), NEG_INF)
        s_c = jnp.where(in_window, _nt_dot(k_cur, q), NEG_INF)
        return s_p, s_c, s_n, _nt_dot(kx_ref[...], q)

    def attend(u, s):
        s_p, s_c, s_n, s_x = s
        _, (v_prev, v_cur, v_next) = operands(u)
        m = jnp.maximum(jnp.maximum(jnp.max(s_p, 0, keepdims=True), jnp.max(s_c, 0, keepdims=True)),
                        jnp.maximum(jnp.max(s_n, 0, keepdims=True), jnp.max(s_x, 0, keepdims=True)))
        m = jnp.maximum(m, sink)
        p_p, p_c, p_n, p_x = jnp.exp2(s_p - m), jnp.exp2(s_c - m), jnp.exp2(s_n - m), jnp.exp2(s_x - m)
        acc = (jnp.dot(v_prev, p_p.astype(BF16), preferred_element_type=F32)
               + jnp.dot(v_cur, p_c.astype(BF16), preferred_element_type=F32)
               + jnp.dot(v_next, p_n.astype(BF16), preferred_element_type=F32)
               + jnp.dot(vx_ref[...], p_x.astype(BF16), preferred_element_type=F32))
        l = acc[HEAD_DIM:HEAD_DIM + 1] + jnp.exp2(sink - m)
        o = (acc[:HEAD_DIM] / l).astype(BF16)
        for g in range(GROUP_B):
            o_ref[g * HEAD_DIM:(g + 1) * HEAD_DIM, u * tq:(u + 1) * tq] = o[:, g * tq:(g + 1) * tq]

    all_scores = [scores(u) for u in range(n_sub)]
    for u in range(n_sub):
        attend(u, all_scores[u])


def _window_call(qkv_x, vt_x, qkv_c, vt_c, sink, *, batch, n, n_ctx):
    assert WINDOW == Q_BLOCK
    nb = WINDOW_QBLOCKS if n % (WINDOW_QBLOCKS * Q_BLOCK) == 0 else 1
    n_sub = WINDOW_SUBTILES if n % (WINDOW_SUBTILES * nb * Q_BLOCK) == 0 else 1
    sub_tq = nb * Q_BLOCK
    nb = nb * n_sub
    tq = nb * Q_BLOCK
    steps = n // tq
    nblk = n // Q_BLOCK

    def edge_block(i, shift):
        return jnp.clip(i * nb + shift, 0, nblk - 1)

    in_specs = [
        pl.BlockSpec((tq, GROUP_B * HEAD_DIM), lambda b, g, i: (b * steps + i, COL_QB // GROUP_B + g)),
        pl.BlockSpec((Q_BLOCK, HEAD_DIM), lambda b, g, i: (b * nblk + edge_block(i, -1), COL_KB + g)),
        pl.BlockSpec((tq, HEAD_DIM), lambda b, g, i: (b * steps + i, COL_KB + g)),
        pl.BlockSpec((Q_BLOCK, HEAD_DIM), lambda b, g, i: (b * nblk + edge_block(i, nb), COL_KB + g)),
        pl.BlockSpec((VT_ROWS, Q_BLOCK), lambda b, g, i: (N_KV_A + g, b * nblk + edge_block(i, -1))),
        pl.BlockSpec((VT_ROWS, tq), lambda b, g, i: (N_KV_A + g, b * steps + i)),
        pl.BlockSpec((VT_ROWS, Q_BLOCK), lambda b, g, i: (N_KV_A + g, b * nblk + edge_block(i, nb))),
        pl.BlockSpec((n_ctx, HEAD_DIM), lambda b, g, i: (b, COL_KB + g)),
        pl.BlockSpec((VT_ROWS, n_ctx), lambda b, g, i: (N_KV_A + g, b)),
        pl.BlockSpec(memory_space=pltpu.SMEM),
    ]
    return pl.pallas_call(
        functools.partial(_window_kernel, nb=sub_tq // Q_BLOCK, n_sub=n_sub),
        grid=(batch, N_KV_B, steps),
        in_specs=in_specs,
        out_specs=pl.BlockSpec((GROUP_B * HEAD_DIM, tq), lambda b, g, i: (g, b * steps + i)),
        out_shape=jax.ShapeDtypeStruct((QB, batch * n), BF16),
        compiler_params=_params("parallel", "parallel", "arbitrary"),
        name="window",
    )(qkv_x, qkv_x, qkv_x, qkv_x, vt_x, vt_x, vt_x, qkv_c, vt_c, sink)


def _out_kernel(*refs, n_in):
    x_ref, mod_ref = refs[:2]
    o_ref = refs[2 + 2 * n_in]
    y = None
    for k in range(n_in):
        part = lax.dot_general(refs[2 + 2 * k][...], refs[3 + 2 * k][...], (((0,), (0,)), ((), ())),
                               preferred_element_type=F32)
        y = part if y is None else y + part
    o_ref[...] = x_ref[...] + _mod_row(mod_ref, 5) * y


def _out_call(x, mod4, pairs, *, l, mod_base, rows_per_mod):
    t, d = x.shape
    tm = min(512, t)
    in_specs = [
        pl.BlockSpec((tm, d), lambda i: (i, 0)),
        pl.BlockSpec((1, 1, N_MOD, d), lambda i: (l, mod_base + (i * tm) // rows_per_mod, 0, 0)),
    ]
    args = [x, mod4]
    for o, w in pairs:
        kk = o.shape[0]
        in_specs += [
            pl.BlockSpec((kk, tm), lambda i: (0, i)),
            pl.BlockSpec((kk, d), lambda i: (0, 0), pipeline_mode=pl.Buffered(1)),
        ]
        args += [o, w]
    return pl.pallas_call(
        functools.partial(_out_kernel, n_in=len(pairs)),
        grid=(t // tm,),
        in_specs=in_specs,
        out_specs=pl.BlockSpec((tm, d), lambda i: (i, 0)),
        out_shape=jax.ShapeDtypeStruct((t, d), F32),
        compiler_params=_params("parallel"),
        name="out_proj",
    )(*args)


def _mla_weight_layout(w_in_c, w_uq, w_ukv):
    d = w_in_c.shape[0]
    w_in = jnp.concatenate([w_in_c, jnp.zeros((d, C_IN_PAD - w_in_c.shape[1]), w_in_c.dtype)], axis=1)
    wq = w_uq.reshape(MLA_Q_RANK, MLA_HEADS, MLA_NOPE + MLA_ROPE)
    wq = jnp.concatenate([wq, jnp.zeros((MLA_Q_RANK, MLA_HEADS, MLA_QK - MLA_NOPE - MLA_ROPE), w_uq.dtype)], axis=-1)
    wkv = w_ukv.reshape(MLA_KV_RANK, MLA_HEADS, MLA_NOPE + MLA_V)
    wk = wkv[:, :, :MLA_NOPE].reshape(MLA_KV_RANK, MLA_HEADS * MLA_NOPE)
    wvt = wkv[:, :, MLA_NOPE:].reshape(MLA_KV_RANK, MLA_HEADS * MLA_V).T
    return (w_in.astype(BF16), wq.reshape(MLA_Q_RANK, MLA_HEADS * MLA_QK).astype(BF16), wk.astype(BF16),
            wvt.astype(BF16))


def kernel(x, c, ctx, c_ctx, w_mod, b_mod, g_norm, w_gate_up, w_down, w_in_ab, g_qnorm_a, g_knorm_a,
           sink_b, w_out_ab, w_in_c, g_cq, g_ckv, w_uq, w_ukv, w_out_c, g_final):
    batch, n, d = x.shape
    n_ctx = ctx.shape[1]
    depth = w_mod.shape[0]
    assert batch + 1 <= 8 and n % GRID_W == 0

    cc = jnp.concatenate([c, c_ctx[None], jnp.zeros((8 - batch - 1, d), F32)], axis=0)
    mod4 = _mod_call(cc, w_mod, b_mod)

    w_gu = w_gate_up.astype(BF16)
    w_dn = w_down.astype(BF16)
    w_ab = w_in_ab.astype(BF16)
    w_oab = w_out_ab.astype(BF16)
    w_oc = w_out_c.astype(BF16)

    tables_h = _rope_tables(n, HEAD_DIM, HEAD_DIM)
    tables_m = _rope_tables(n, MLA_ROPE, 128)

    xs = x.reshape(batch * n, d)
    hs = ctx.reshape(batch * n_ctx, d)
    x_mod = dict(mod_base=0, rows_per_mod=n)
    c_mod = dict(mod_base=batch, rows_per_mod=batch * n_ctx)

    for l in range(depth):
        need_ctx = l < depth - 1
        last = l == depth - 1
        i = l // 2
        xs = _ffn_call(xs, mod4, g_norm, w_gu, w_dn, l=l, jf=0, jm=0, **x_mod)
        hs = _ffn_call(hs, mod4, g_norm, w_gu, w_dn, l=l, jf=0, jm=0, **c_mod)
        if l % 2 == 0:
            w = w_ab[i]
            w_qk = jnp.concatenate([w[:, :W_KA], w[:, W_QB:W_KB], w[:, W_KA:W_VA], w[:, W_KB:W_VB]], axis=1)
            w_vt = jnp.concatenate([w[:, W_VA:W_QB], w[:, W_VB:]], axis=1).T
            qkv_x, vt_x = _proj_ab_call(xs, mod4, g_norm, w_qk, w_vt, g_qnorm_a[i], g_knorm_a[i], tables_h,
                                        l=l, n_pos=n, **x_mod)
            qkv_c, vt_c = _proj_ab_call(hs, mod4, g_norm, w_qk, w_vt, g_qnorm_a[i], g_knorm_a[i], None,
                                        l=l, n_pos=n, **c_mod)
            oa = _flash_call(qkv_x, (qkv_c, vt_c), (qkv_x, vt_x), None, batch=batch, n_q=n, n_ctx=n_ctx,
                             n_heads=N_HEADS_A, group=GROUP_A, dq=HEAD_DIM, dv=HEAD_DIM,
                             q_col=COL_QA, k_col=COL_KA, v_col=0, tq=min(256, n))
            ob = _window_call(qkv_x, vt_x, qkv_c, vt_c, sink_b[i], batch=batch, n=n, n_ctx=n_ctx)
            w_o = w_oab[i]
            xs_new = _out_call(xs, mod4, [(oa, w_o[:QA]), (ob, w_o[QA:])], l=l, **x_mod)
            if need_ctx:
                oa_c = _flash_call(qkv_c, (qkv_c, vt_c), None, None, batch=batch, n_q=n_ctx, n_ctx=n_ctx,
                                   n_heads=N_HEADS_A, group=GROUP_A, dq=HEAD_DIM, dv=HEAD_DIM,
                                   q_col=COL_QA, k_col=COL_KA, v_col=0, tq=min(128, n_ctx))
                ob_c = _flash_call(qkv_c, (qkv_c, vt_c), None, sink_b[i], batch=batch, n_q=n_ctx, n_ctx=n_ctx,
                                   n_heads=N_HEADS_B, group=GROUP_B, dq=HEAD_DIM, dv=HEAD_DIM,
                                   q_col=COL_QB, k_col=COL_KB, v_col=N_KV_A, tq=min(128, n_ctx))
                hs = _out_call(hs, mod4, [(oa_c, w_o[:QA]), (ob_c, w_o[QA:])], l=l, **c_mod)
            xs = xs_new
        else:
            w_in, wq, wk, wvt = _mla_weight_layout(w_in_c[i], w_uq[i], w_ukv[i])
            q_x, k_x, v_x = _proj_mla_call(xs, mod4, g_norm, w_in, g_cq[i], g_ckv[i], wq, wk, wvt, tables_m,
                                           l=l, n_pos=n, **x_mod)
            q_c, k_c, v_c = _proj_mla_call(hs, mod4, g_norm, w_in, g_cq[i], g_ckv[i], wq, wk, wvt, None,
                                           l=l, n_pos=n, **c_mod)
            o = _flash_call(q_x, (k_c, v_c), (k_x, v_x), None, batch=batch, n_q=n, n_ctx=n_ctx,
                            n_heads=MLA_HEADS, group=1, dq=MLA_QK, dv=MLA_V, q_col=0, k_col=0, v_col=0,
                            tq=min(1024, n))
            xs_new = _out_call(xs, mod4, [(o, w_oc[i])], l=l, **x_mod)
            if need_ctx:
                o_c = _flash_call(q_c, (k_c, v_c), None, None, batch=batch, n_q=n_ctx, n_ctx=n_ctx,
                                  n_heads=MLA_HEADS, group=1, dq=MLA_QK, dv=MLA_V, q_col=0, k_col=0, v_col=0,
                                  tq=n_ctx)
                hs = _out_call(hs, mod4, [(o_c, w_oc[i])], l=l, **c_mod)
            xs = xs_new
        xs = _ffn_call(xs, mod4, g_norm, w_gu, w_dn, l=l, jf=1, jm=2, g_final=g_final if last else None, **x_mod)
        if need_ctx:
            hs = _ffn_call(hs, mod4, g_norm, w_gu, w_dn, l=l, jf=1, jm=2, **c_mod)
    return xs.reshape(batch, n, d)
```

```python
import functools

import jax
import jax.numpy as jnp
import numpy as np
from jax import lax
from jax.experimental import pallas as pl
from jax.experimental.pallas import tpu as pltpu

GRID_W = 64
HEAD_DIM = 128
N_HEADS_A = 8
N_KV_A = 2
N_HEADS_B = 8
N_KV_B = 2
WINDOW = 128
Q_BLOCK = 128
ROPE_THETA = 10000.0
MLA_HEADS = 16
MLA_Q_RANK = 512
MLA_KV_RANK = 256
MLA_NOPE = 128
MLA_ROPE = 64
MLA_V = 128
MLA_QK = 256
N_MOD = 9
EPS = 1e-6
NEG_INF = -1e30
LOG2E = 1.4426950408889634

QA = N_HEADS_A * HEAD_DIM
KA = N_KV_A * HEAD_DIM
QB = N_HEADS_B * HEAD_DIM
KB = N_KV_B * HEAD_DIM
GROUP_A = N_HEADS_A // N_KV_A
GROUP_B = N_HEADS_B // N_KV_B
W_KA = QA
W_VA = W_KA + KA
W_QB = W_VA + KA
W_KB = W_QB + QB
W_VB = W_KB + KB
COL_QA = 0
COL_QB = N_HEADS_A
COL_KA = COL_QB + N_HEADS_B
COL_KB = COL_KA + N_KV_A
N_COLS_QK = COL_KB + N_KV_B
QK_IN = N_COLS_QK * HEAD_DIM
C_IN_PAD = MLA_Q_RANK + MLA_KV_RANK + 128

F32 = jnp.float32
BF16 = jnp.bfloat16
V7X_VMEM_LIMIT_BYTES = 56 * 1024 * 1024
FLASH_UNROLL = 4
WINDOW_QBLOCKS = 2
WINDOW_SUBTILES = 4
FFN_FIRST_PARTS = 2
ADALN_ROWS = 16
DOT_COLS = 512
ONES_ROWS = 16
VT_ROWS = HEAD_DIM + ONES_ROWS


def _params(*sem):
    return pltpu.CompilerParams(dimension_semantics=sem, vmem_limit_bytes=V7X_VMEM_LIMIT_BYTES)


def _rms(x):
    return x * lax.rsqrt(jnp.mean(x * x, axis=-1, keepdims=True) + EPS)


def _silu(x):
    return x / (1.0 + jnp.exp(-x))


def _mod_row(mod_ref, k):
    return mod_ref[0, 0, k:k + 1, :]


def _adaln(x, g_ref, jn, mod_ref, j):
    return _rms(x) * g_ref[0, jn:jn + 1, :] * (1.0 + _mod_row(mod_ref, 3 * j + 1)) + _mod_row(mod_ref, 3 * j)


def _adaln_rows_static(x_ref, out_ref, r0, n_rows, g_ref, jn, mod_ref, j):
    gain = g_ref[0, jn:jn + 1, :] * (1.0 + _mod_row(mod_ref, 3 * j + 1))
    shift = _mod_row(mod_ref, 3 * j)
    for r in range(r0, r0 + n_rows, ADALN_ROWS):
        out_ref[r:r + ADALN_ROWS, :] = (_rms(x_ref[r:r + ADALN_ROWS, :]) * gain + shift).astype(BF16)


def _mod_kernel(c_ref, w_ref, b_ref, o_ref):
    sc = _silu(c_ref[...]).astype(BF16)
    o_ref[0] = jnp.dot(sc, w_ref[0].astype(BF16), preferred_element_type=F32) + b_ref[0]


def _mod_call(cc, w_mod, b_mod):
    depth, d, nd = w_mod.shape
    tn = min(d, 1024)
    out = pl.pallas_call(
        _mod_kernel,
        grid=(depth, nd // tn),
        in_specs=[
            pl.BlockSpec((8, d), lambda l, j: (0, 0)),
            pl.BlockSpec((1, d, tn), lambda l, j: (l, 0, j)),
            pl.BlockSpec((1, 1, tn), lambda l, j: (l, 0, j)),
        ],
        out_specs=pl.BlockSpec((1, 8, tn), lambda l, j: (l, 0, j)),
        out_shape=jax.ShapeDtypeStruct((depth, 8, nd), F32),
        compiler_params=_params("parallel", "parallel"),
        name="mod",
    )(cc, w_mod, b_mod.reshape(depth, 1, nd))
    return out.reshape(depth, 8, N_MOD, d)


def _ffn_kernel(*refs, jm, jn, final, single_step):
    if final:
        x_ref, mod_ref, g_ref, wg_ref, wu_ref, wd_ref, gf_ref, o_ref, xn_ref = refs
    else:
        x_ref, mod_ref, g_ref, wg_ref, wu_ref, wd_ref, o_ref, xn_ref = refs
    j = pl.program_id(1)
    n_steps = pl.num_programs(1)

    tm = x_ref.shape[0]

    def chunk(first, last, r0, rows):
        rs = slice(r0, r0 + rows)
        if first:
            _adaln_rows_static(x_ref, xn_ref, r0, rows, g_ref, jn, mod_ref, jm)
        xn = xn_ref[rs, :]
        gt = jnp.dot(xn, wg_ref[0, 0], preferred_element_type=F32)
        up = jnp.dot(xn, wu_ref[0, 0], preferred_element_type=F32)
        h = (_silu(gt) * up).astype(BF16)
        part = jnp.dot(h, wd_ref[0, 0], preferred_element_type=F32)
        acc = part if first else o_ref[rs, :] + part
        if last:
            acc = x_ref[rs, :] + (0.5 * _mod_row(mod_ref, 3 * jm + 2)) * acc
            if final:
                acc = _rms(acc) * gf_ref[...]
        o_ref[rs, :] = acc

    def step(first, last):
        n_parts = FFN_FIRST_PARTS if first and tm % (FFN_FIRST_PARTS * ADALN_ROWS) == 0 else 1
        for part_idx in range(n_parts):
            chunk(first, last, part_idx * (tm // n_parts), tm // n_parts)

    if single_step:
        step(True, True)
    else:
        pl.when(j == 0)(lambda: step(True, False))
        pl.when((j > 0) & (j < n_steps - 1))(lambda: step(False, False))
        pl.when(j == n_steps - 1)(lambda: step(False, True))


def _ffn_call(x, mod4, g_norm, w_gu, w_dn, *, l, jf, jm, mod_base, rows_per_mod, g_final=None):
    t, d = x.shape
    f = w_dn.shape[2]
    tm = min(512, t)
    tf = min(512, f)
    nf = f // tf
    final = g_final is not None
    in_specs = [
        pl.BlockSpec((tm, d), lambda i, j: (i, 0)),
        pl.BlockSpec((1, 1, N_MOD, d), lambda i, j: (l, mod_base + (i * tm) // rows_per_mod, 0, 0)),
        pl.BlockSpec((1, 3, d), lambda i, j: (l, 0, 0)),
        pl.BlockSpec((1, 1, d, tf), lambda i, j: (l, jf, 0, j)),
        pl.BlockSpec((1, 1, d, tf), lambda i, j: (l, jf, 0, nf + j)),
        pl.BlockSpec((1, 1, tf, d), lambda i, j: (l, jf, j, 0)),
    ]
    args = [x, mod4, g_norm, w_gu, w_gu, w_dn]
    if final:
        in_specs.append(pl.BlockSpec((1, d), lambda i, j: (0, 0)))
        args.append(g_final.reshape(1, d))
    return pl.pallas_call(
        functools.partial(_ffn_kernel, jm=jm, jn=jm, final=final, single_step=nf == 1),
        grid=(t // tm, nf),
        in_specs=in_specs,
        out_specs=pl.BlockSpec((tm, d), lambda i, j: (i, 0)),
        out_shape=jax.ShapeDtypeStruct((t, d), F32),
        scratch_shapes=[pltpu.VMEM((tm, d), BF16)],
        compiler_params=_params("parallel", "arbitrary"),
        name="ffn",
    )(*args)


def _rope_tables(n_pos, rot_dim, width):
    nf = rot_dim // 4
    pos = np.arange(n_pos)
    inv = np.float32(ROPE_THETA) ** (-np.arange(nf, dtype=np.float32) / np.float32(nf))
    ang_r = (pos // GRID_W).astype(np.float32)[:, None] * inv
    ang_c = (pos % GRID_W).astype(np.float32)[:, None] * inv
    pad = np.zeros((n_pos, width - rot_dim), np.float32)
    cos = np.concatenate([np.cos(ang_r), np.cos(ang_r), np.cos(ang_c), np.cos(ang_c), pad], axis=-1)
    sin = np.concatenate([-np.sin(ang_r), np.sin(ang_r), -np.sin(ang_c), np.sin(ang_c), pad], axis=-1)
    return jnp.asarray(cos, F32), jnp.asarray(sin, F32)


def _rotate(y, cos, sin, nf):
    lanes = y.shape[-1]
    lane = lax.broadcasted_iota(jnp.int32, y.shape, 1)
    partner = jnp.where(lane % (2 * nf) < nf, pltpu.roll(y, lanes - nf, 1), pltpu.roll(y, nf, 1))
    return y * cos + partner * sin


def _store_vt(vt_ref, vt, n_heads):
    ones = jnp.ones((ONES_ROWS, vt.shape[1]), BF16)
    for h in range(n_heads):
        vt_ref[h * VT_ROWS:h * VT_ROWS + HEAD_DIM, :] = vt[h * HEAD_DIM:(h + 1) * HEAD_DIM].astype(BF16)
        vt_ref[h * VT_ROWS + HEAD_DIM:(h + 1) * VT_ROWS, :] = ones


def _proj_ab_kernel(*refs, rope):
    if rope:
        x_ref, mod_ref, g_ref, w_ref, wvt_ref, gq_ref, gk_ref, cos_ref, sin_ref, o_ref, vt_ref = refs
        cos, sin = cos_ref[...], sin_ref[...]
    else:
        x_ref, mod_ref, g_ref, w_ref, wvt_ref, gq_ref, gk_ref, o_ref, vt_ref = refs
    xn = _adaln(x_ref[...], g_ref, 1, mod_ref, 1).astype(BF16)
    scale = HEAD_DIM ** -0.5 * LOG2E
    hpd = DOT_COLS // HEAD_DIM
    for c in range(N_COLS_QK // hpd):
        yc = jnp.dot(xn, w_ref[:, c * DOT_COLS:(c + 1) * DOT_COLS], preferred_element_type=F32)
        for hh in range(hpd):
            col = hpd * c + hh
            y = yc[:, hh * HEAD_DIM:(hh + 1) * HEAD_DIM]
            is_q = col < COL_KA
            if col < COL_QB:
                y = _rms(y) * gq_ref[...]
            if COL_KA <= col < COL_KB:
                y = _rms(y) * gk_ref[...]
            if rope:
                y = _rotate(y, cos, sin, HEAD_DIM // 4)
            if is_q:
                y = y * scale
            o_ref[:, col * HEAD_DIM:(col + 1) * HEAD_DIM] = y.astype(BF16)
    _store_vt(vt_ref, _nt_dot(wvt_ref[...], xn), N_KV_A + N_KV_B)


def _proj_ab_call(x, mod4, g_norm, w_in, w_vt, g_q, g_k, tables, *, l, mod_base, rows_per_mod, n_pos):
    t, d = x.shape
    tm = min(512, t)
    rope = tables is not None
    const = dict(pipeline_mode=pl.Buffered(1))
    in_specs = [
        pl.BlockSpec((tm, d), lambda i: (i, 0)),
        pl.BlockSpec((1, 1, N_MOD, d), lambda i: (l, mod_base + (i * tm) // rows_per_mod, 0, 0)),
        pl.BlockSpec((1, 3, d), lambda i: (l, 0, 0)),
        pl.BlockSpec((d, QK_IN), lambda i: (0, 0), **const),
        pl.BlockSpec((KA + KB, d), lambda i: (0, 0), **const),
        pl.BlockSpec((1, HEAD_DIM), lambda i: (0, 0)),
        pl.BlockSpec((1, HEAD_DIM), lambda i: (0, 0)),
    ]
    args = [x, mod4, g_norm, w_in, w_vt, g_q.reshape(1, HEAD_DIM), g_k.reshape(1, HEAD_DIM)]
    if rope:
        npb = n_pos // tm
        in_specs += [pl.BlockSpec((tm, HEAD_DIM), lambda i: (i % npb, 0))] * 2
        args += list(tables)
    return pl.pallas_call(
        functools.partial(_proj_ab_kernel, rope=rope),
        grid=(t // tm,),
        in_specs=in_specs,
        out_specs=[pl.BlockSpec((tm, QK_IN), lambda i: (i, 0)),
                   pl.BlockSpec(((N_KV_A + N_KV_B) * VT_ROWS, tm), lambda i: (0, i))],
        out_shape=[jax.ShapeDtypeStruct((t, QK_IN), BF16),
                   jax.ShapeDtypeStruct(((N_KV_A + N_KV_B) * VT_ROWS, t), BF16)],
        compiler_params=_params("parallel"),
        name="proj_ab",
    )(*args)


def _proj_mla_kernel(*refs, rope):
    if rope:
        (x_ref, mod_ref, g_ref, wi_ref, gcq_ref, gckv_ref, wuq_ref, wuk_ref, wuvt_ref, cos_ref, sin_ref,
         q_ref, k_ref, vt_ref) = refs
        cos, sin = cos_ref[...], sin_ref[...]
    else:
        x_ref, mod_ref, g_ref, wi_ref, gcq_ref, gckv_ref, wuq_ref, wuk_ref, wuvt_ref, q_ref, k_ref, vt_ref = refs
    xn = _adaln(x_ref[...], g_ref, 1, mod_ref, 1).astype(BF16)
    t = jnp.dot(xn, wi_ref[...], preferred_element_type=F32)
    cq = (_rms(t[:, :MLA_Q_RANK]) * gcq_ref[...]).astype(BF16)
    ckv = (_rms(t[:, MLA_Q_RANK:MLA_Q_RANK + MLA_KV_RANK]) * gckv_ref[...]).astype(BF16)
    kr = t[:, MLA_Q_RANK + MLA_KV_RANK:]
    if rope:
        kr = _rotate(kr, cos, sin, MLA_ROPE // 4)
    kr = kr.astype(BF16)
    scale = (MLA_NOPE + MLA_ROPE) ** -0.5 * LOG2E
    q_heads = DOT_COLS // MLA_QK
    for hp in range(MLA_HEADS // q_heads):
        qc = jnp.dot(cq, wuq_ref[:, hp * DOT_COLS:(hp + 1) * DOT_COLS], preferred_element_type=F32)
        for hh in range(q_heads):
            h = q_heads * hp + hh
            qn = qc[:, hh * MLA_QK:hh * MLA_QK + MLA_NOPE]
            qr = qc[:, hh * MLA_QK + MLA_NOPE:(hh + 1) * MLA_QK]
            if rope:
                qr = _rotate(qr, cos, sin, MLA_ROPE // 4)
            q_ref[:, h * MLA_QK:h * MLA_QK + MLA_NOPE] = (qn * scale).astype(BF16)
            q_ref[:, h * MLA_QK + MLA_NOPE:(h + 1) * MLA_QK] = (qr * scale).astype(BF16)
            k_ref[:, h * MLA_QK + MLA_NOPE:(h + 1) * MLA_QK] = kr
    k_heads = DOT_COLS // MLA_NOPE
    for hp in range(MLA_HEADS // k_heads):
        kn = jnp.dot(ckv, wuk_ref[:, hp * DOT_COLS:(hp + 1) * DOT_COLS], preferred_element_type=F32)
        for hh in range(k_heads):
            h = k_heads * hp + hh
            k_ref[:, h * MLA_QK:h * MLA_QK + MLA_NOPE] = kn[:, hh * MLA_NOPE:(hh + 1) * MLA_NOPE].astype(BF16)
    _store_vt(vt_ref, _nt_dot(wuvt_ref[...], ckv), MLA_HEADS)


def _proj_mla_call(x, mod4, g_norm, w_in, g_cq, g_ckv, w_uq, w_uk, w_uvt, tables, *, l, mod_base, rows_per_mod,
                   n_pos):
    t, d = x.shape
    tm = min(512, t)
    rope = tables is not None
    const = dict(pipeline_mode=pl.Buffered(1))
    in_specs = [
        pl.BlockSpec((tm, d), lambda i: (i, 0)),
        pl.BlockSpec((1, 1, N_MOD, d), lambda i: (l, mod_base + (i * tm) // rows_per_mod, 0, 0)),
        pl.BlockSpec((1, 3, d), lambda i: (l, 0, 0)),
        pl.BlockSpec((d, C_IN_PAD), lambda i: (0, 0), **const),
        pl.BlockSpec((1, MLA_Q_RANK), lambda i: (0, 0)),
        pl.BlockSpec((1, MLA_KV_RANK), lambda i: (0, 0)),
        pl.BlockSpec((MLA_Q_RANK, MLA_HEADS * MLA_QK), lambda i: (0, 0), **const),
        pl.BlockSpec((MLA_KV_RANK, MLA_HEADS * MLA_NOPE), lambda i: (0, 0), **const),
        pl.BlockSpec((MLA_HEADS * MLA_V, MLA_KV_RANK), lambda i: (0, 0), **const),
    ]
    args = [x, mod4, g_norm, w_in, g_cq.reshape(1, -1), g_ckv.reshape(1, -1), w_uq, w_uk, w_uvt]
    if rope:
        npb = n_pos // tm
        in_specs += [pl.BlockSpec((tm, 128), lambda i: (i % npb, 0))] * 2
        args += list(tables)
    return pl.pallas_call(
        functools.partial(_proj_mla_kernel, rope=rope),
        grid=(t // tm,),
        in_specs=in_specs,
        out_specs=[
            pl.BlockSpec((tm, MLA_HEADS * MLA_QK), lambda i: (i, 0)),
            pl.BlockSpec((tm, MLA_HEADS * MLA_QK), lambda i: (i, 0)),
            pl.BlockSpec((MLA_HEADS * VT_ROWS, tm), lambda i: (0, i)),
        ],
        out_shape=[
            jax.ShapeDtypeStruct((t, MLA_HEADS * MLA_QK), BF16),
            jax.ShapeDtypeStruct((t, MLA_HEADS * MLA_QK), BF16),
            jax.ShapeDtypeStruct((MLA_HEADS * VT_ROWS, t), BF16),
        ],
        compiler_params=_params("parallel"),
        name="proj_mla",
    )(*args)


def _nt_dot(a, b):
    return lax.dot_general(a, b, (((1,), (1,)), ((), ())), preferred_element_type=F32)


def _flash_kernel(*refs, group, dq, dv, tk, n_lat, has_sink, unroll):
    refs = list(refs)
    q_ref, kc_ref, vc_ref = refs[:3]
    pos = 3
    if n_lat:
        kl_ref, vl_ref = refs[pos:pos + 2]
        pos += 2
    if has_sink:
        sink_ref = refs[pos]
        pos += 1
    o_ref = refs[pos]
    if n_lat:
        s_ref = refs[pos + 1]
    tq = q_ref.shape[0]
    q = jnp.concatenate([q_ref[:, g * dq:(g + 1) * dq] for g in range(group)], axis=0) if group > 1 else q_ref[...]

    def scores(t, slot):
        s_ref[slot] = _nt_dot(kl_ref[pl.ds(pl.multiple_of(t * tk, tk), tk), :], q)

    s = _nt_dot(kc_ref[...], q)
    if n_lat:
        scores(0, 0)
    m = jnp.max(s, axis=0, keepdims=True)
    if has_sink:
        head0 = pl.program_id(1) * group
        sink = jnp.concatenate(
            [jnp.full((1, tq), sink_ref[head0 + g] * LOG2E, F32) for g in range(group)], axis=1)
        m = jnp.maximum(m, sink)
    p = jnp.exp2(s - m)
    acc = jnp.dot(vc_ref[...], p.astype(BF16), preferred_element_type=F32)

    if n_lat:
        def update(t, slot, m, acc):
            s = s_ref[slot]
            m_new = jnp.maximum(m, jnp.max(s, axis=0, keepdims=True))
            alpha = jnp.exp2(m - m_new)
            p = jnp.exp2(s - m_new)
            r0 = pl.multiple_of(t * tk, tk)
            acc = alpha * acc + jnp.dot(vl_ref[:, pl.ds(r0, tk)], p.astype(BF16), preferred_element_type=F32)
            return m_new, acc

        def pair(tt, carry):
            t0 = 2 * tt
            scores(t0 + 1, 1)
            carry = update(t0, 0, *carry)
            scores(jnp.minimum(t0 + 2, n_lat - 1), 0)
            return update(t0 + 1, 1, *carry)

        m, acc = lax.fori_loop(0, n_lat // 2, pair, (m, acc), unroll=unroll)
        if n_lat % 2:
            m, acc = update(n_lat - 1, 0, m, acc)

    l = acc[dv:dv + 1]
    if has_sink:
        l = l + jnp.exp2(sink - m)
    o = (acc[:dv] / l).astype(BF16)
    for g in range(group):
        o_ref[g * dv:(g + 1) * dv, :] = o[:, g * tq:(g + 1) * tq]


def _flash_call(q_arr, kv_ctx, kv_lat, sink, *, batch, n_q, n_ctx, n_heads, group, dq, dv, q_col, k_col, v_col, tq):
    n_kv = n_heads // group
    nqb = n_q // tq
    has_sink = sink is not None
    in_specs = [
        pl.BlockSpec((tq, group * dq), lambda b, g, i: (b * nqb + i, q_col // group + g)),
        pl.BlockSpec((n_ctx, dq), lambda b, g, i: (b, k_col + g)),
        pl.BlockSpec((dv + ONES_ROWS, n_ctx), lambda b, g, i: (v_col + g, b)),
    ]
    args = [q_arr, kv_ctx[0], kv_ctx[1]]
    n_lat = 0
    tk = 0
    if kv_lat is not None:
        n_rows = kv_lat[0].shape[0] // batch
        tk = min(512, n_rows)
        n_lat = n_rows // tk
        in_specs += [
            pl.BlockSpec((n_rows, dq), lambda b, g, i: (b, k_col + g)),
            pl.BlockSpec((dv + ONES_ROWS, n_rows), lambda b, g, i: (v_col + g, b)),
        ]
        args += [kv_lat[0], kv_lat[1]]
    if has_sink:
        in_specs.append(pl.BlockSpec(memory_space=pltpu.SMEM))
        args.append(sink)
    return pl.pallas_call(
        functools.partial(_flash_kernel, group=group, dq=dq, dv=dv, tk=tk, n_lat=n_lat, has_sink=has_sink,
                          unroll=max(1, min(FLASH_UNROLL, n_lat // 2))),
        grid=(batch, n_kv, nqb),
        scratch_shapes=[pltpu.VMEM((2, tk, group * tq), F32)] if n_lat else [],
        in_specs=in_specs,
        out_specs=pl.BlockSpec((group * dv, tq), lambda b, g, i: (g, b * nqb + i)),
        out_shape=jax.ShapeDtypeStruct((n_heads * dv, batch * n_q), BF16),
        compiler_params=_params("parallel", "parallel", "arbitrary"),
        name="flash",
    )(*args)


def _window_kernel(q_ref, kp_ref, kc_ref, kn_ref, vp_ref, vc_ref, vn_ref, kx_ref, vx_ref, sink_ref, o_ref, *, nb,
                   n_sub):
    i = pl.program_id(2)
    last = pl.num_programs(2) - 1
    tq = nb * Q_BLOCK
    cols = GROUP_B * tq
    head0 = pl.program_id(1) * GROUP_B
    sink = jnp.concatenate([jnp.full((1, tq), sink_ref[head0 + g] * LOG2E, F32) for g in range(GROUP_B)], axis=1)

    def qpos(rows):
        return lax.broadcasted_iota(jnp.int32, (rows, cols), 1) % tq

    def kpos(rows):
        return lax.broadcasted_iota(jnp.int32, (rows, cols), 0)

    r_e, c_e = qpos(Q_BLOCK), kpos(Q_BLOCK)
    in_window = jnp.abs(qpos(tq) - kpos(tq)) <= WINDOW
    off_p = jnp.where(i > 0, 0, tq + Q_BLOCK)
    off_n = jnp.where(i < last, 0, tq + Q_BLOCK)

    def operands(u):
        lo, hi = u * tq, (u + 1) * tq
        k_prev = kp_ref[...] if u == 0 else kc_ref[lo - Q_BLOCK:lo, :]
        v_prev = vp_ref[...] if u == 0 else vc_ref[:, lo - Q_BLOCK:lo]
        k_next = kn_ref[...] if u == n_sub - 1 else kc_ref[hi:hi + Q_BLOCK, :]
        v_next = vn_ref[...] if u == n_sub - 1 else vc_ref[:, hi:hi + Q_BLOCK]
        return (k_prev, kc_ref[lo:hi, :], k_next), (v_prev, vc_ref[:, lo:hi], v_next)

    def scores(u):
        q = jnp.concatenate([q_ref[u * tq:(u + 1) * tq, g * HEAD_DIM:(g + 1) * HEAD_DIM] for g in range(GROUP_B)],
                            axis=0)
        (k_prev, k_cur, k_next), _ = operands(u)
        s_p = jnp.where(r_e + (off_p if u == 0 else 0) <= c_e, _nt_dot(k_prev, q), NEG_INF)
        s_n = jnp.where(r_e >= c_e + (tq - WINDOW) + (off_n if u == n_sub - 1 else 0), _nt_dot(k_next, q), NEG_INF)
        s_c = jnp.where(in_window, _nt_dot(k_cur, q), NEG_INF)
        return s_p, s_c, s_n, _nt_dot(kx_ref[...], q)

    def attend(u, s):
        s_p, s_c, s_n, s_x = s
        _, (v_prev, v_cur, v_next) = operands(u)
        m = jnp.maximum(jnp.maximum(jnp.max(s_p, 0, keepdims=True), jnp.max(s_c, 0, keepdims=True)),
                        jnp.maximum(jnp.max(s_n, 0, keepdims=True), jnp.max(s_x, 0, keepdims=True)))
        m = jnp.maximum(m, sink)
        p_p, p_c, p_n, p_x = jnp.exp2(s_p - m), jnp.exp2(s_c - m), jnp.exp2(s_n - m), jnp.exp2(s_x - m)
        acc = (jnp.dot(v_prev, p_p.astype(BF16), preferred_element_type=F32)
               + jnp.dot(v_cur, p_c.astype(BF16), preferred_element_type=F32)
               + jnp.dot(v_next, p_n.astype(BF16), preferred_element_type=F32)
               + jnp.dot(vx_ref[...], p_x.astype(BF16), preferred_element_type=F32))
        l = acc[HEAD_DIM:HEAD_DIM + 1] + jnp.exp2(sink - m)
        o = (acc[:HEAD_DIM] / l).astype(BF16)
        for g in range(GROUP_B):
            o_ref[g * HEAD_DIM:(g + 1) * HEAD_DIM, u * tq:(u + 1) * tq] = o[:, g * tq:(g + 1) * tq]

    all_scores = [scores(u) for u in range(n_sub)]
    for u in range(n_sub):
        attend(u, all_scores[u])


def _window_call(qkv_x, vt_x, qkv_c, vt_c, sink, *, batch, n, n_ctx):
    assert WINDOW == Q_BLOCK
    nb = WINDOW_QBLOCKS if n % (WINDOW_QBLOCKS * Q_BLOCK) == 0 else 1
    n_sub = WINDOW_SUBTILES if n % (WINDOW_SUBTILES * nb * Q_BLOCK) == 0 else 1
    sub_tq = nb * Q_BLOCK
    nb = nb * n_sub
    tq = nb * Q_BLOCK
    steps = n // tq
    nblk = n // Q_BLOCK

    def edge_block(i, shift):
        return jnp.clip(i * nb + shift, 0, nblk - 1)

    in_specs = [
        pl.BlockSpec((tq, GROUP_B * HEAD_DIM), lambda b, g, i: (b * steps + i, COL_QB // GROUP_B + g)),
        pl.BlockSpec((Q_BLOCK, HEAD_DIM), lambda b, g, i: (b * nblk + edge_block(i, -1), COL_KB + g)),
        pl.BlockSpec((tq, HEAD_DIM), lambda b, g, i: (b * steps + i, COL_KB + g)),
        pl.BlockSpec((Q_BLOCK, HEAD_DIM), lambda b, g, i: (b * nblk + edge_block(i, nb), COL_KB + g)),
        pl.BlockSpec((VT_ROWS, Q_BLOCK), lambda b, g, i: (N_KV_A + g, b * nblk + edge_block(i, -1))),
        pl.BlockSpec((VT_ROWS, tq), lambda b, g, i: (N_KV_A + g, b * steps + i)),
        pl.BlockSpec((VT_ROWS, Q_BLOCK), lambda b, g, i: (N_KV_A + g, b * nblk + edge_block(i, nb))),
        pl.BlockSpec((n_ctx, HEAD_DIM), lambda b, g, i: (b, COL_KB + g)),
        pl.BlockSpec((VT_ROWS, n_ctx), lambda b, g, i: (N_KV_A + g, b)),
        pl.BlockSpec(memory_space=pltpu.SMEM),
    ]
    return pl.pallas_call(
        functools.partial(_window_kernel, nb=sub_tq // Q_BLOCK, n_sub=n_sub),
        grid=(batch, N_KV_B, steps),
        in_specs=in_specs,
        out_specs=pl.BlockSpec((GROUP_B * HEAD_DIM, tq), lambda b, g, i: (g, b * steps + i)),
        out_shape=jax.ShapeDtypeStruct((QB, batch * n), BF16),
        compiler_params=_params("parallel", "parallel", "arbitrary"),
        name="window",
    )(qkv_x, qkv_x, qkv_x, qkv_x, vt_x, vt_x, vt_x, qkv_c, vt_c, sink)


def _out_kernel(*refs, n_in):
    x_ref, mod_ref = refs[:2]
    o_ref = refs[2 + 2 * n_in]
    y = None
    for k in range(n_in):
        part = lax.dot_general(refs[2 + 2 * k][...], refs[3 + 2 * k][...], (((0,), (0,)), ((), ())),
                               preferred_element_type=F32)
        y = part if y is None else y + part
    o_ref[...] = x_ref[...] + _mod_row(mod_ref, 5) * y


def _out_call(x, mod4, pairs, *, l, mod_base, rows_per_mod):
    t, d = x.shape
    tm = min(512, t)
    in_specs = [
        pl.BlockSpec((tm, d), lambda i: (i, 0)),
        pl.BlockSpec((1, 1, N_MOD, d), lambda i: (l, mod_base + (i * tm) // rows_per_mod, 0, 0)),
    ]
    args = [x, mod4]
    for o, w in pairs:
        kk = o.shape[0]
        in_specs += [
            pl.BlockSpec((kk, tm), lambda i: (0, i)),
            pl.BlockSpec((kk, d), lambda i: (0, 0), pipeline_mode=pl.Buffered(1)),
        ]
        args += [o, w]
    return pl.pallas_call(
        functools.partial(_out_kernel, n_in=len(pairs)),
        grid=(t // tm,),
        in_specs=in_specs,
        out_specs=pl.BlockSpec((tm, d), lambda i: (i, 0)),
        out_shape=jax.ShapeDtypeStruct((t, d), F32),
        compiler_params=_params("parallel"),
        name="out_proj",
    )(*args)


def _mla_weight_layout(w_in_c, w_uq, w_ukv):
    d = w_in_c.shape[0]
    w_in = jnp.concatenate([w_in_c, jnp.zeros((d, C_IN_PAD - w_in_c.shape[1]), w_in_c.dtype)], axis=1)
    wq = w_uq.reshape(MLA_Q_RANK, MLA_HEADS, MLA_NOPE + MLA_ROPE)
    wq = jnp.concatenate([wq, jnp.zeros((MLA_Q_RANK, MLA_HEADS, MLA_QK - MLA_NOPE - MLA_ROPE), w_uq.dtype)], axis=-1)
    wkv = w_ukv.reshape(MLA_KV_RANK, MLA_HEADS, MLA_NOPE + MLA_V)
    wk = wkv[:, :, :MLA_NOPE].reshape(MLA_KV_RANK, MLA_HEADS * MLA_NOPE)
    wvt = wkv[:, :, MLA_NOPE:].reshape(MLA_KV_RANK, MLA_HEADS * MLA_V).T
    return (w_in.astype(BF16), wq.reshape(MLA_Q_RANK, MLA_HEADS * MLA_QK).astype(BF16), wk.astype(BF16),
            wvt.astype(BF16))


def kernel(x, c, ctx, c_ctx, w_mod, b_mod, g_norm, w_gate_up, w_down, w_in_ab, g_qnorm_a, g_knorm_a,
           sink_b, w_out_ab, w_in_c, g_cq, g_ckv, w_uq, w_ukv, w_out_c, g_final):
    batch, n, d = x.shape
    n_ctx = ctx.shape[1]
    depth = w_mod.shape[0]
    assert batch + 1 <= 8 and n % GRID_W == 0

    cc = jnp.concatenate([c, c_ctx[None], jnp.zeros((8 - batch - 1, d), F32)], axis=0)
    mod4 = _mod_call(cc, w_mod, b_mod)

    w_gu = w_gate_up.astype(BF16)
    w_dn = w_down.astype(BF16)
    w_ab = w_in_ab.astype(BF16)
    w_oab = w_out_ab.astype(BF16)
    w_oc = w_out_c.astype(BF16)

    tables_h = _rope_tables(n, HEAD_DIM, HEAD_DIM)
    tables_m = _rope_tables(n, MLA_ROPE, 128)

    xs = x.reshape(batch * n, d)
    hs = ctx.reshape(batch * n_ctx, d)
    x_mod = dict(mod_base=0, rows_per_mod=n)
    c_mod = dict(mod_base=batch, rows_per_mod=batch * n_ctx)

    for l in range(depth):
        need_ctx = l < depth - 1
        last = l == depth - 1
        i = l // 2
        xs = _ffn_call(xs, mod4, g_norm, w_gu, w_dn, l=l, jf=0, jm=0, **x_mod)
        hs = _ffn_call(hs, mod4, g_norm, w_gu, w_dn, l=l, jf=0, jm=0, **c_mod)
        if l % 2 == 0:
            w = w_ab[i]
            w_qk = jnp.concatenate([w[:, :W_KA], w[:, W_QB:W_KB], w[:, W_KA:W_VA], w[:, W_KB:W_VB]], axis=1)
            w_vt = jnp.concatenate([w[:, W_VA:W_QB], w[:, W_VB:]], axis=1).T
            qkv_x, vt_x = _proj_ab_call(xs, mod4, g_norm, w_qk, w_vt, g_qnorm_a[i], g_knorm_a[i], tables_h,
                                        l=l, n_pos=n, **x_mod)
            qkv_c, vt_c = _proj_ab_call(hs, mod4, g_norm, w_qk, w_vt, g_qnorm_a[i], g_knorm_a[i], None,
                                        l=l, n_pos=n, **c_mod)
            oa = _flash_call(qkv_x, (qkv_c, vt_c), (qkv_x, vt_x), None, batch=batch, n_q=n, n_ctx=n_ctx,
                             n_heads=N_HEADS_A, group=GROUP_A, dq=HEAD_DIM, dv=HEAD_DIM,
                             q_col=COL_QA, k_col=COL_KA, v_col=0, tq=min(256, n))
            ob = _window_call(qkv_x, vt_x, qkv_c, vt_c, sink_b[i], batch=batch, n=n, n_ctx=n_ctx)
            w_o = w_oab[i]
            xs_new = _out_call(xs, mod4, [(oa, w_o[:QA]), (ob, w_o[QA:])], l=l, **x_mod)
            if need_ctx:
                oa_c = _flash_call(qkv_c, (qkv_c, vt_c), None, None, batch=batch, n_q=n_ctx, n_ctx=n_ctx,
                                   n_heads=N_HEADS_A, group=GROUP_A, dq=HEAD_DIM, dv=HEAD_DIM,
                                   q_col=COL_QA, k_col=COL_KA, v_col=0, tq=min(128, n_ctx))
                ob_c = _flash_call(qkv_c, (qkv_c, vt_c), None, sink_b[i], batch=batch, n_q=n_ctx, n_ctx=n_ctx,
                                   n_heads=N_HEADS_B, group=GROUP_B, dq=HEAD_DIM, dv=HEAD_DIM,
                                   q_col=COL_QB, k_col=COL_KB, v_col=N_KV_A, tq=min(128, n_ctx))
                hs = _out_call(hs, mod4, [(oa_c, w_o[:QA]), (ob_c, w_o[QA:])], l=l, **c_mod)
            xs = xs_new
        else:
            w_in, wq, wk, wvt = _mla_weight_layout(w_in_c[i], w_uq[i], w_ukv[i])
            q_x, k_x, v_x = _proj_mla_call(xs, mod4, g_norm, w_in, g_cq[i], g_ckv[i], wq, wk, wvt, tables_m,
                                           l=l, n_pos=n, **x_mod)
            q_c, k_c, v_c = _proj_mla_call(hs, mod4, g_norm, w_in, g_cq[i], g_ckv[i], wq, wk, wvt, None,
                                           l=l, n_pos=n, **c_mod)
            o = _flash_call(q_x, (k_c, v_c), (k_x, v_x), None, batch=batch, n_q=n, n_ctx=n_ctx,
                            n_heads=MLA_HEADS, group=1, dq=MLA_QK, dv=MLA_V, q_col=0, k_col=0, v_col=0,
                            tq=min(1024, n))
            xs_new = _out_call(xs, mod4, [(o, w_oc[i])], l=l, **x_mod)
            if need_ctx:
                o_c = _flash_call(q_c, (k_c, v_c), None, None, batch=batch, n_q=n_ctx, n_ctx=n_ctx,
                                  n_heads=MLA_HEADS, group=1, dq=MLA_QK, dv=MLA_V, q_col=0, k_col=0, v_col=0,
                                  tq=n_ctx)
                hs = _out_call(hs, mod4, [(o_c, w_oc[i])], l=l, **c_mod)
            xs = xs_new
        xs = _ffn_call(xs, mod4, g_norm, w_gu, w_dn, l=l, jf=1, jm=2, g_final=g_final if last else None, **x_mod)
        if need_ctx:
            hs = _ffn_call(hs, mod4, g_norm, w_gu, w_dn, l=l, jf=1, jm=2, **c_mod)
    return xs.reshape(batch, n, d)
```

```python
import functools

import jax
import jax.numpy as jnp
import numpy as np
from jax import lax
from jax.experimental import pallas as pl
from jax.experimental.pallas import tpu as pltpu

GRID_W = 64
HEAD_DIM = 128
N_HEADS_A = 8
N_KV_A = 2
N_HEADS_B = 8
N_KV_B = 2
WINDOW = 128
Q_BLOCK = 128
ROPE_THETA = 10000.0
MLA_HEADS = 16
MLA_Q_RANK = 512
MLA_KV_RANK = 256
MLA_NOPE = 128
MLA_ROPE = 64
MLA_V = 128
MLA_QK = 256
N_MOD = 9
EPS = 1e-6
NEG_INF = -1e30
LOG2E = 1.4426950408889634

QA = N_HEADS_A * HEAD_DIM
KA = N_KV_A * HEAD_DIM
QB = N_HEADS_B * HEAD_DIM
KB = N_KV_B * HEAD_DIM
GROUP_A = N_HEADS_A // N_KV_A
GROUP_B = N_HEADS_B // N_KV_B
W_KA = QA
W_VA = W_KA + KA
W_QB = W_VA + KA
W_KB = W_QB + QB
W_VB = W_KB + KB
COL_QA = 0
COL_QB = N_HEADS_A
COL_KA = COL_QB + N_HEADS_B
COL_KB = COL_KA + N_KV_A
N_COLS_QK = COL_KB + N_KV_B
QK_IN = N_COLS_QK * HEAD_DIM
C_IN_PAD = MLA_Q_RANK + MLA_KV_RANK + 128

F32 = jnp.float32
BF16 = jnp.bfloat16
V7X_VMEM_LIMIT_BYTES = 56 * 1024 * 1024
FLASH_UNROLL = 4
WINDOW_QBLOCKS = 2
WINDOW_SUBTILES = 4
PROJ_ROW_PARTS = 2
FFN_FIRST_PARTS = 2
ADALN_ROWS = 16
DOT_COLS = 512
ONES_ROWS = 16
VT_ROWS = HEAD_DIM + ONES_ROWS


def _params(*sem):
    return pltpu.CompilerParams(dimension_semantics=sem, vmem_limit_bytes=V7X_VMEM_LIMIT_BYTES)


def _rms(x):
    return x * lax.rsqrt(jnp.mean(x * x, axis=-1, keepdims=True) + EPS)


def _silu(x):
    return x / (1.0 + jnp.exp(-x))


def _mod_row(mod_ref, k):
    return mod_ref[0, 0, k:k + 1, :]


def _adaln(x, g_ref, jn, mod_ref, j):
    return _rms(x) * g_ref[0, jn:jn + 1, :] * (1.0 + _mod_row(mod_ref, 3 * j + 1)) + _mod_row(mod_ref, 3 * j)


def _adaln_rows_static(x_ref, out_ref, r0, n_rows, g_ref, jn, mod_ref, j):
    gain = g_ref[0, jn:jn + 1, :] * (1.0 + _mod_row(mod_ref, 3 * j + 1))
    shift = _mod_row(mod_ref, 3 * j)
    for r in range(r0, r0 + n_rows, ADALN_ROWS):
        out_ref[r:r + ADALN_ROWS, :] = (_rms(x_ref[r:r + ADALN_ROWS, :]) * gain + shift).astype(BF16)


def _mod_kernel(c_ref, w_ref, b_ref, o_ref):
    sc = _silu(c_ref[...]).astype(BF16)
    o_ref[0] = jnp.dot(sc, w_ref[0].astype(BF16), preferred_element_type=F32) + b_ref[0]


def _mod_call(cc, w_mod, b_mod):
    depth, d, nd = w_mod.shape
    tn = min(d, 1024)
    out = pl.pallas_call(
        _mod_kernel,
        grid=(depth, nd // tn),
        in_specs=[
            pl.BlockSpec((8, d), lambda l, j: (0, 0)),
            pl.BlockSpec((1, d, tn), lambda l, j: (l, 0, j)),
            pl.BlockSpec((1, 1, tn), lambda l, j: (l, 0, j)),
        ],
        out_specs=pl.BlockSpec((1, 8, tn), lambda l, j: (l, 0, j)),
        out_shape=jax.ShapeDtypeStruct((depth, 8, nd), F32),
        compiler_params=_params("parallel", "parallel"),
        name="mod",
    )(cc, w_mod, b_mod.reshape(depth, 1, nd))
    return out.reshape(depth, 8, N_MOD, d)


def _ffn_kernel(*refs, jm, jn, final, single_step):
    if final:
        x_ref, mod_ref, g_ref, wg_ref, wu_ref, wd_ref, gf_ref, o_ref, xn_ref = refs
    else:
        x_ref, mod_ref, g_ref, wg_ref, wu_ref, wd_ref, o_ref, xn_ref = refs
    j = pl.program_id(1)
    n_steps = pl.num_programs(1)

    tm = x_ref.shape[0]

    def chunk(first, last, r0, rows):
        rs = slice(r0, r0 + rows)
        if first:
            _adaln_rows_static(x_ref, xn_ref, r0, rows, g_ref, jn, mod_ref, jm)
        xn = xn_ref[rs, :]
        gt = jnp.dot(xn, wg_ref[0, 0], preferred_element_type=F32)
        up = jnp.dot(xn, wu_ref[0, 0], preferred_element_type=F32)
        h = (_silu(gt) * up).astype(BF16)
        part = jnp.dot(h, wd_ref[0, 0], preferred_element_type=F32)
        acc = part if first else o_ref[rs, :] + part
        if last:
            acc = x_ref[rs, :] + (0.5 * _mod_row(mod_ref, 3 * jm + 2)) * acc
            if final:
                acc = _rms(acc) * gf_ref[...]
        o_ref[rs, :] = acc

    def step(first, last):
        n_parts = FFN_FIRST_PARTS if first and tm % (FFN_FIRST_PARTS * ADALN_ROWS) == 0 else 1
        for part_idx in range(n_parts):
            chunk(first, last, part_idx * (tm // n_parts), tm // n_parts)

    if single_step:
        step(True, True)
    else:
        pl.when(j == 0)(lambda: step(True, False))
        pl.when((j > 0) & (j < n_steps - 1))(lambda: step(False, False))
        pl.when(j == n_steps - 1)(lambda: step(False, True))


def _ffn_call(x, mod4, g_norm, w_gu, w_dn, *, l, jf, jm, mod_base, rows_per_mod, g_final=None):
    t, d = x.shape
    f = w_dn.shape[2]
    tm = min(512, t)
    tf = min(512, f)
    nf = f // tf
    final = g_final is not None
    in_specs = [
        pl.BlockSpec((tm, d), lambda i, j: (i, 0)),
        pl.BlockSpec((1, 1, N_MOD, d), lambda i, j: (l, mod_base + (i * tm) // rows_per_mod, 0, 0)),
        pl.BlockSpec((1, 3, d), lambda i, j: (l, 0, 0)),
        pl.BlockSpec((1, 1, d, tf), lambda i, j: (l, jf, 0, j)),
        pl.BlockSpec((1, 1, d, tf), lambda i, j: (l, jf, 0, nf + j)),
        pl.BlockSpec((1, 1, tf, d), lambda i, j: (l, jf, j, 0)),
    ]
    args = [x, mod4, g_norm, w_gu, w_gu, w_dn]
    if final:
        in_specs.append(pl.BlockSpec((1, d), lambda i, j: (0, 0)))
        args.append(g_final.reshape(1, d))
    return pl.pallas_call(
        functools.partial(_ffn_kernel, jm=jm, jn=jm, final=final, single_step=nf == 1),
        grid=(t // tm, nf),
        in_specs=in_specs,
        out_specs=pl.BlockSpec((tm, d), lambda i, j: (i, 0)),
        out_shape=jax.ShapeDtypeStruct((t, d), F32),
        scratch_shapes=[pltpu.VMEM((tm, d), BF16)],
        compiler_params=_params("parallel", "arbitrary"),
        name="ffn",
    )(*args)


def _rope_tables(n_pos, rot_dim, width):
    nf = rot_dim // 4
    pos = np.arange(n_pos)
    inv = np.float32(ROPE_THETA) ** (-np.arange(nf, dtype=np.float32) / np.float32(nf))
    ang_r = (pos // GRID_W).astype(np.float32)[:, None] * inv
    ang_c = (pos % GRID_W).astype(np.float32)[:, None] * inv
    pad = np.zeros((n_pos, width - rot_dim), np.float32)
    cos = np.concatenate([np.cos(ang_r), np.cos(ang_r), np.cos(ang_c), np.cos(ang_c), pad], axis=-1)
    sin = np.concatenate([-np.sin(ang_r), np.sin(ang_r), -np.sin(ang_c), np.sin(ang_c), pad], axis=-1)
    return jnp.asarray(cos, F32), jnp.asarray(sin, F32)


def _rotate(y, cos, sin, nf):
    lanes = y.shape[-1]
    lane = lax.broadcasted_iota(jnp.int32, y.shape, 1)
    partner = jnp.where(lane % (2 * nf) < nf, pltpu.roll(y, lanes - nf, 1), pltpu.roll(y, nf, 1))
    return y * cos + partner * sin


def _store_vt(vt_ref, vt, n_heads, cols=slice(None)):
    ones = jnp.ones((ONES_ROWS, vt.shape[1]), BF16)
    for h in range(n_heads):
        vt_ref[h * VT_ROWS:h * VT_ROWS + HEAD_DIM, cols] = vt[h * HEAD_DIM:(h + 1) * HEAD_DIM].astype(BF16)
        vt_ref[h * VT_ROWS + HEAD_DIM:(h + 1) * VT_ROWS, cols] = ones


def _proj_ab_kernel(*refs, rope):
    if rope:
        x_ref, mod_ref, g_ref, w_ref, wvt_ref, gq_ref, gk_ref, cos_ref, sin_ref, o_ref, vt_ref = refs
    else:
        x_ref, mod_ref, g_ref, w_ref, wvt_ref, gq_ref, gk_ref, o_ref, vt_ref = refs
    scale = HEAD_DIM ** -0.5 * LOG2E
    hpd = DOT_COLS // HEAD_DIM
    tm = x_ref.shape[0]
    n_parts = PROJ_ROW_PARTS if tm % (PROJ_ROW_PARTS * 128) == 0 else 1
    for part in range(n_parts):
        rs = slice(part * (tm // n_parts), (part + 1) * (tm // n_parts))
        xn = _adaln(x_ref[rs, :], g_ref, 1, mod_ref, 1).astype(BF16)
        if rope:
            cos, sin = cos_ref[rs, :], sin_ref[rs, :]
        for c in range(N_COLS_QK // hpd):
            yc = jnp.dot(xn, w_ref[:, c * DOT_COLS:(c + 1) * DOT_COLS], preferred_element_type=F32)
            for hh in range(hpd):
                col = hpd * c + hh
                y = yc[:, hh * HEAD_DIM:(hh + 1) * HEAD_DIM]
                is_q = col < COL_KA
                if col < COL_QB:
                    y = _rms(y) * gq_ref[...]
                if COL_KA <= col < COL_KB:
                    y = _rms(y) * gk_ref[...]
                if rope:
                    y = _rotate(y, cos, sin, HEAD_DIM // 4)
                if is_q:
                    y = y * scale
                o_ref[rs, col * HEAD_DIM:(col + 1) * HEAD_DIM] = y.astype(BF16)
        _store_vt(vt_ref, _nt_dot(wvt_ref[...], xn), N_KV_A + N_KV_B, rs)


def _proj_ab_call(x, mod4, g_norm, w_in, w_vt, g_q, g_k, tables, *, l, mod_base, rows_per_mod, n_pos):
    t, d = x.shape
    tm = min(512, t)
    rope = tables is not None
    const = dict(pipeline_mode=pl.Buffered(1))
    in_specs = [
        pl.BlockSpec((tm, d), lambda i: (i, 0)),
        pl.BlockSpec((1, 1, N_MOD, d), lambda i: (l, mod_base + (i * tm) // rows_per_mod, 0, 0)),
        pl.BlockSpec((1, 3, d), lambda i: (l, 0, 0)),
        pl.BlockSpec((d, QK_IN), lambda i: (0, 0), **const),
        pl.BlockSpec((KA + KB, d), lambda i: (0, 0), **const),
        pl.BlockSpec((1, HEAD_DIM), lambda i: (0, 0)),
        pl.BlockSpec((1, HEAD_DIM), lambda i: (0, 0)),
    ]
    args = [x, mod4, g_norm, w_in, w_vt, g_q.reshape(1, HEAD_DIM), g_k.reshape(1, HEAD_DIM)]
    if rope:
        npb = n_pos // tm
        in_specs += [pl.BlockSpec((tm, HEAD_DIM), lambda i: (i % npb, 0))] * 2
        args += list(tables)
    return pl.pallas_call(
        functools.partial(_proj_ab_kernel, rope=rope),
        grid=(t // tm,),
        in_specs=in_specs,
        out_specs=[pl.BlockSpec((tm, QK_IN), lambda i: (i, 0)),
                   pl.BlockSpec(((N_KV_A + N_KV_B) * VT_ROWS, tm), lambda i: (0, i))],
        out_shape=[jax.ShapeDtypeStruct((t, QK_IN), BF16),
                   jax.ShapeDtypeStruct(((N_KV_A + N_KV_B) * VT_ROWS, t), BF16)],
        compiler_params=_params("parallel"),
        name="proj_ab",
    )(*args)


def _proj_mla_kernel(*refs, rope):
    if rope:
        (x_ref, mod_ref, g_ref, wi_ref, gcq_ref, gckv_ref, wuq_ref, wuk_ref, wuvt_ref, cos_ref, sin_ref,
         q_ref, k_ref, vt_ref) = refs
        cos, sin = cos_ref[...], sin_ref[...]
    else:
        x_ref, mod_ref, g_ref, wi_ref, gcq_ref, gckv_ref, wuq_ref, wuk_ref, wuvt_ref, q_ref, k_ref, vt_ref = refs
    xn = _adaln(x_ref[...], g_ref, 1, mod_ref, 1).astype(BF16)
    t = jnp.dot(xn, wi_ref[...], preferred_element_type=F32)
    cq = (_rms(t[:, :MLA_Q_RANK]) * gcq_ref[...]).astype(BF16)
    ckv = (_rms(t[:, MLA_Q_RANK:MLA_Q_RANK + MLA_KV_RANK]) * gckv_ref[...]).astype(BF16)
    kr = t[:, MLA_Q_RANK + MLA_KV_RANK:]
    if rope:
        kr = _rotate(kr, cos, sin, MLA_ROPE // 4)
    kr = kr.astype(BF16)
    scale = (MLA_NOPE + MLA_ROPE) ** -0.5 * LOG2E
    q_heads = DOT_COLS // MLA_QK
    for hp in range(MLA_HEADS // q_heads):
        qc = jnp.dot(cq, wuq_ref[:, hp * DOT_COLS:(hp + 1) * DOT_COLS], preferred_element_type=F32)
        for hh in range(q_heads):
            h = q_heads * hp + hh
            qn = qc[:, hh * MLA_QK:hh * MLA_QK + MLA_NOPE]
            qr = qc[:, hh * MLA_QK + MLA_NOPE:(hh + 1) * MLA_QK]
            if rope:
                qr = _rotate(qr, cos, sin, MLA_ROPE // 4)
            q_ref[:, h * MLA_QK:h * MLA_QK + MLA_NOPE] = (qn * scale).astype(BF16)
            q_ref[:, h * MLA_QK + MLA_NOPE:(h + 1) * MLA_QK] = (qr * scale).astype(BF16)
            k_ref[:, h * MLA_QK + MLA_NOPE:(h + 1) * MLA_QK] = kr
    k_heads = DOT_COLS // MLA_NOPE
    for hp in range(MLA_HEADS // k_heads):
        kn = jnp.dot(ckv, wuk_ref[:, hp * DOT_COLS:(hp + 1) * DOT_COLS], preferred_element_type=F32)
        for hh in range(k_heads):
            h = k_heads * hp + hh
            k_ref[:, h * MLA_QK:h * MLA_QK + MLA_NOPE] = kn[:, hh * MLA_NOPE:(hh + 1) * MLA_NOPE].astype(BF16)
    _store_vt(vt_ref, _nt_dot(wuvt_ref[...], ckv), MLA_HEADS)


def _proj_mla_call(x, mod4, g_norm, w_in, g_cq, g_ckv, w_uq, w_uk, w_uvt, tables, *, l, mod_base, rows_per_mod,
                   n_pos):
    t, d = x.shape
    tm = min(512, t)
    rope = tables is not None
    const = dict(pipeline_mode=pl.Buffered(1))
    in_specs = [
        pl.BlockSpec((tm, d), lambda i: (i, 0)),
        pl.BlockSpec((1, 1, N_MOD, d), lambda i: (l, mod_base + (i * tm) // rows_per_mod, 0, 0)),
        pl.BlockSpec((1, 3, d), lambda i: (l, 0, 0)),
        pl.BlockSpec((d, C_IN_PAD), lambda i: (0, 0), **const),
        pl.BlockSpec((1, MLA_Q_RANK), lambda i: (0, 0)),
        pl.BlockSpec((1, MLA_KV_RANK), lambda i: (0, 0)),
        pl.BlockSpec((MLA_Q_RANK, MLA_HEADS * MLA_QK), lambda i: (0, 0), **const),
        pl.BlockSpec((MLA_KV_RANK, MLA_HEADS * MLA_NOPE), lambda i: (0, 0), **const),
        pl.BlockSpec((MLA_HEADS * MLA_V, MLA_KV_RANK), lambda i: (0, 0), **const),
    ]
    args = [x, mod4, g_norm, w_in, g_cq.reshape(1, -1), g_ckv.reshape(1, -1), w_uq, w_uk, w_uvt]
    if rope:
        npb = n_pos // tm
        in_specs += [pl.BlockSpec((tm, 128), lambda i: (i % npb, 0))] * 2
        args += list(tables)
    return pl.pallas_call(
        functools.partial(_proj_mla_kernel, rope=rope),
        grid=(t // tm,),
        in_specs=in_specs,
        out_specs=[
            pl.BlockSpec((tm, MLA_HEADS * MLA_QK), lambda i: (i, 0)),
            pl.BlockSpec((tm, MLA_HEADS * MLA_QK), lambda i: (i, 0)),
            pl.BlockSpec((MLA_HEADS * VT_ROWS, tm), lambda i: (0, i)),
        ],
        out_shape=[
            jax.ShapeDtypeStruct((t, MLA_HEADS * MLA_QK), BF16),
            jax.ShapeDtypeStruct((t, MLA_HEADS * MLA_QK), BF16),
            jax.ShapeDtypeStruct((MLA_HEADS * VT_ROWS, t), BF16),
        ],
        compiler_params=_params("parallel"),
        name="proj_mla",
    )(*args)


def _nt_dot(a, b):
    return lax.dot_general(a, b, (((1,), (1,)), ((), ())), preferred_element_type=F32)


def _flash_kernel(*refs, group, dq, dv, tk, n_lat, has_sink, unroll):
    refs = list(refs)
    q_ref, kc_ref, vc_ref = refs[:3]
    pos = 3
    if n_lat:
        kl_ref, vl_ref = refs[pos:pos + 2]
        pos += 2
    if has_sink:
        sink_ref = refs[pos]
        pos += 1
    o_ref = refs[pos]
    if n_lat:
        s_ref = refs[pos + 1]
    tq = q_ref.shape[0]
    q = jnp.concatenate([q_ref[:, g * dq:(g + 1) * dq] for g in range(group)], axis=0) if group > 1 else q_ref[...]

    def scores(t, slot):
        s_ref[slot] = _nt_dot(kl_ref[pl.ds(pl.multiple_of(t * tk, tk), tk), :], q)

    s = _nt_dot(kc_ref[...], q)
    if n_lat:
        scores(0, 0)
    m = jnp.max(s, axis=0, keepdims=True)
    if has_sink:
        head0 = pl.program_id(1) * group
        sink = jnp.concatenate(
            [jnp.full((1, tq), sink_ref[head0 + g] * LOG2E, F32) for g in range(group)], axis=1)
        m = jnp.maximum(m, sink)
    p = jnp.exp2(s - m)
    acc = jnp.dot(vc_ref[...], p.astype(BF16), preferred_element_type=F32)

    if n_lat:
        def update(t, slot, m, acc):
            s = s_ref[slot]
            m_new = jnp.maximum(m, jnp.max(s, axis=0, keepdims=True))
            alpha = jnp.exp2(m - m_new)
            p = jnp.exp2(s - m_new)
            r0 = pl.multiple_of(t * tk, tk)
            acc = alpha * acc + jnp.dot(vl_ref[:, pl.ds(r0, tk)], p.astype(BF16), preferred_element_type=F32)
            return m_new, acc

        def pair(tt, carry):
            t0 = 2 * tt
            scores(t0 + 1, 1)
            carry = update(t0, 0, *carry)
            scores(jnp.minimum(t0 + 2, n_lat - 1), 0)
            return update(t0 + 1, 1, *carry)

        m, acc = lax.fori_loop(0, n_lat // 2, pair, (m, acc), unroll=unroll)
        if n_lat % 2:
            m, acc = update(n_lat - 1, 0, m, acc)

    l = acc[dv:dv + 1]
    if has_sink:
        l = l + jnp.exp2(sink - m)
    o = (acc[:dv] / l).astype(BF16)
    for g in range(group):
        o_ref[g * dv:(g + 1) * dv, :] = o[:, g * tq:(g + 1) * tq]


def _flash_call(q_arr, kv_ctx, kv_lat, sink, *, batch, n_q, n_ctx, n_heads, group, dq, dv, q_col, k_col, v_col, tq):
    n_kv = n_heads // group
    nqb = n_q // tq
    has_sink = sink is not None
    in_specs = [
        pl.BlockSpec((tq, group * dq), lambda b, g, i: (b * nqb + i, q_col // group + g)),
        pl.BlockSpec((n_ctx, dq), lambda b, g, i: (b, k_col + g)),
        pl.BlockSpec((dv + ONES_ROWS, n_ctx), lambda b, g, i: (v_col + g, b)),
    ]
    args = [q_arr, kv_ctx[0], kv_ctx[1]]
    n_lat = 0
    tk = 0
    if kv_lat is not None:
        n_rows = kv_lat[0].shape[0] // batch
        tk = min(512, n_rows)
        n_lat = n_rows // tk
        in_specs += [
            pl.BlockSpec((n_rows, dq), lambda b, g, i: (b, k_col + g)),
            pl.BlockSpec((dv + ONES_ROWS, n_rows), lambda b, g, i: (v_col + g, b)),
        ]
        args += [kv_lat[0], kv_lat[1]]
    if has_sink:
        in_specs.append(pl.BlockSpec(memory_space=pltpu.SMEM))
        args.append(sink)
    return pl.pallas_call(
        functools.partial(_flash_kernel, group=group, dq=dq, dv=dv, tk=tk, n_lat=n_lat, has_sink=has_sink,
                          unroll=max(1, min(FLASH_UNROLL, n_lat // 2))),
        grid=(batch, n_kv, nqb),
        scratch_shapes=[pltpu.VMEM((2, tk, group * tq), F32)] if n_lat else [],
        in_specs=in_specs,
        out_specs=pl.BlockSpec((group * dv, tq), lambda b, g, i: (g, b * nqb + i)),
        out_shape=jax.ShapeDtypeStruct((n_heads * dv, batch * n_q), BF16),
        compiler_params=_params("parallel", "parallel", "arbitrary"),
        name="flash",
    )(*args)


def _window_kernel(q_ref, kp_ref, kc_ref, kn_ref, vp_ref, vc_ref, vn_ref, kx_ref, vx_ref, sink_ref, o_ref, *, nb,
                   n_sub):
    i = pl.program_id(2)
    last = pl.num_programs(2) - 1
    tq = nb * Q_BLOCK
    cols = GROUP_B * tq
    head0 = pl.program_id(1) * GROUP_B
    sink = jnp.concatenate([jnp.full((1, tq), sink_ref[head0 + g] * LOG2E, F32) for g in range(GROUP_B)], axis=1)

    def qpos(rows):
        return lax.broadcasted_iota(jnp.int32, (rows, cols), 1) % tq

    def kpos(rows):
        return lax.broadcasted_iota(jnp.int32, (rows, cols), 0)

    r_e, c_e = qpos(Q_BLOCK), kpos(Q_BLOCK)
    in_window = jnp.abs(qpos(tq) - kpos(tq)) <= WINDOW
    off_p = jnp.where(i > 0, 0, tq + Q_BLOCK)
    off_n = jnp.where(i < last, 0, tq + Q_BLOCK)

    def operands(u):
        lo, hi = u * tq, (u + 1) * tq
        k_prev = kp_ref[...] if u == 0 else kc_ref[lo - Q_BLOCK:lo, :]
        v_prev = vp_ref[...] if u == 0 else vc_ref[:, lo - Q_BLOCK:lo]
        k_next = kn_ref[...] if u == n_sub - 1 else kc_ref[hi:hi + Q_BLOCK, :]
        v_next = vn_ref[...] if u == n_sub - 1 else vc_ref[:, hi:hi + Q_BLOCK]
        return (k_prev, kc_ref[lo:hi, :], k_next), (v_prev, vc_ref[:, lo:hi], v_next)

    def scores(u):
        q = jnp.concatenate([q_ref[u * tq:(u + 1) * tq, g * HEAD_DIM:(g + 1) * HEAD_DIM] for g in range(GROUP_B)],
                            axis=0)
        (k_prev, k_cur, k_next), _ = operands(u)
        s_p = jnp.where(r_e + (off_p if u == 0 else 0) <= c_e, _nt_dot(k_prev, q), NEG_INF)
        s_n = jnp.where(r_e >= c_e + (tq - WINDOW) + (off_n if u == n_sub - 1 else 0), _nt_dot(k_next, q), NEG_INF)
        s_c = jnp.where(in_window, _nt_dot(k_cur, q), NEG_INF)
        return s_p, s_c, s_n, _nt_dot(kx_ref[...], q)

    def attend(u, s):
        s_p, s_c, s_n, s_x = s
        _, (v_prev, v_cur, v_next) = operands(u)
        m = jnp.maximum(jnp.maximum(jnp.max(s_p, 0, keepdims=True), jnp.max(s_c, 0, keepdims=True)),
                        jnp.maximum(jnp.max(s_n, 0, keepdims=True), jnp.max(s_x, 0, keepdims=True)))
        m = jnp.maximum(m, sink)
        p_p, p_c, p_n, p_x = jnp.exp2(s_p - m), jnp.exp2(s_c - m), jnp.exp2(s_n - m), jnp.exp2(s_x - m)
        acc = (jnp.dot(v_prev, p_p.astype(BF16), preferred_element_type=F32)
               + jnp.dot(v_cur, p_c.astype(BF16), preferred_element_type=F32)
               + jnp.dot(v_next, p_n.astype(BF16), preferred_element_type=F32)
               + jnp.dot(vx_ref[...], p_x.astype(BF16), preferred_element_type=F32))
        l = acc[HEAD_DIM:HEAD_DIM + 1] + jnp.exp2(sink - m)
        o = (acc[:HEAD_DIM] / l).astype(BF16)
        for g in range(GROUP_B):
            o_ref[g * HEAD_DIM:(g + 1) * HEAD_DIM, u * tq:(u + 1) * tq] = o[:, g * tq:(g + 1) * tq]

    all_scores = [scores(u) for u in range(n_sub)]
    for u in range(n_sub):
        attend(u, all_scores[u])


def _window_call(qkv_x, vt_x, qkv_c, vt_c, sink, *, batch, n, n_ctx):
    assert WINDOW == Q_BLOCK
    nb = WINDOW_QBLOCKS if n % (WINDOW_QBLOCKS * Q_BLOCK) == 0 else 1
    n_sub = WINDOW_SUBTILES if n % (WINDOW_SUBTILES * nb * Q_BLOCK) == 0 else 1
    sub_tq = nb * Q_BLOCK
    nb = nb * n_sub
    tq = nb * Q_BLOCK
    steps = n // tq
    nblk = n // Q_BLOCK

    def edge_block(i, shift):
        return jnp.clip(i * nb + shift, 0, nblk - 1)

    in_specs = [
        pl.BlockSpec((tq, GROUP_B * HEAD_DIM), lambda b, g, i: (b * steps + i, COL_QB // GROUP_B + g)),
        pl.BlockSpec((Q_BLOCK, HEAD_DIM), lambda b, g, i: (b * nblk + edge_block(i, -1), COL_KB + g)),
        pl.BlockSpec((tq, HEAD_DIM), lambda b, g, i: (b * steps + i, COL_KB + g)),
        pl.BlockSpec((Q_BLOCK, HEAD_DIM), lambda b, g, i: (b * nblk + edge_block(i, nb), COL_KB + g)),
        pl.BlockSpec((VT_ROWS, Q_BLOCK), lambda b, g, i: (N_KV_A + g, b * nblk + edge_block(i, -1))),
        pl.BlockSpec((VT_ROWS, tq), lambda b, g, i: (N_KV_A + g, b * steps + i)),
        pl.BlockSpec((VT_ROWS, Q_BLOCK), lambda b, g, i: (N_KV_A + g, b * nblk + edge_block(i, nb))),
        pl.BlockSpec((n_ctx, HEAD_DIM), lambda b, g, i: (b, COL_KB + g)),
        pl.BlockSpec((VT_ROWS, n_ctx), lambda b, g, i: (N_KV_A + g, b)),
        pl.BlockSpec(memory_space=pltpu.SMEM),
    ]
    return pl.pallas_call(
        functools.partial(_window_kernel, nb=sub_tq // Q_BLOCK, n_sub=n_sub),
        grid=(batch, N_KV_B, steps),
        in_specs=in_specs,
        out_specs=pl.BlockSpec((GROUP_B * HEAD_DIM, tq), lambda b, g, i: (g, b * steps + i)),
        out_shape=jax.ShapeDtypeStruct((QB, batch * n), BF16),
        compiler_params=_params("parallel", "parallel", "arbitrary"),
        name="window",
    )(qkv_x, qkv_x, qkv_x, qkv_x, vt_x, vt_x, vt_x, qkv_c, vt_c, sink)


def _out_kernel(*refs, n_in):
    x_ref, mod_ref = refs[:2]
    o_ref = refs[2 + 2 * n_in]
    y = None
    for k in range(n_in):
        part = lax.dot_general(refs[2 + 2 * k][...], refs[3 + 2 * k][...], (((0,), (0,)), ((), ())),
                               preferred_element_type=F32)
        y = part if y is None else y + part
    o_ref[...] = x_ref[...] + _mod_row(mod_ref, 5) * y


def _out_call(x, mod4, pairs, *, l, mod_base, rows_per_mod):
    t, d = x.shape
    tm = min(512, t)
    in_specs = [
        pl.BlockSpec((tm, d), lambda i: (i, 0)),
        pl.BlockSpec((1, 1, N_MOD, d), lambda i: (l, mod_base + (i * tm) // rows_per_mod, 0, 0)),
    ]
    args = [x, mod4]
    for o, w in pairs:
        kk = o.shape[0]
        in_specs += [
            pl.BlockSpec((kk, tm), lambda i: (0, i)),
            pl.BlockSpec((kk, d), lambda i: (0, 0), pipeline_mode=pl.Buffered(1)),
        ]
        args += [o, w]
    return pl.pallas_call(
        functools.partial(_out_kernel, n_in=len(pairs)),
        grid=(t // tm,),
        in_specs=in_specs,
        out_specs=pl.BlockSpec((tm, d), lambda i: (i, 0)),
        out_shape=jax.ShapeDtypeStruct((t, d), F32),
        compiler_params=_params("parallel"),
        name="out_proj",
    )(*args)


def _mla_weight_layout(w_in_c, w_uq, w_ukv):
    d = w_in_c.shape[0]
    w_in = jnp.concatenate([w_in_c, jnp.zeros((d, C_IN_PAD - w_in_c.shape[1]), w_in_c.dtype)], axis=1)
    wq = w_uq.reshape(MLA_Q_RANK, MLA_HEADS, MLA_NOPE + MLA_ROPE)
    wq = jnp.concatenate([wq, jnp.zeros((MLA_Q_RANK, MLA_HEADS, MLA_QK - MLA_NOPE - MLA_ROPE), w_uq.dtype)], axis=-1)
    wkv = w_ukv.reshape(MLA_KV_RANK, MLA_HEADS, MLA_NOPE + MLA_V)
    wk = wkv[:, :, :MLA_NOPE].reshape(MLA_KV_RANK, MLA_HEADS * MLA_NOPE)
    wvt = wkv[:, :, MLA_NOPE:].reshape(MLA_KV_RANK, MLA_HEADS * MLA_V).T
    return (w_in.astype(BF16), wq.reshape(MLA_Q_RANK, MLA_HEADS * MLA_QK).astype(BF16), wk.astype(BF16),
            wvt.astype(BF16))


def kernel(x, c, ctx, c_ctx, w_mod, b_mod, g_norm, w_gate_up, w_down, w_in_ab, g_qnorm_a, g_knorm_a,
           sink_b, w_out_ab, w_in_c, g_cq, g_ckv, w_uq, w_ukv, w_out_c, g_final):
    batch, n, d = x.shape
    n_ctx = ctx.shape[1]
    depth = w_mod.shape[0]
    assert batch + 1 <= 8 and n % GRID_W == 0

    cc = jnp.concatenate([c, c_ctx[None], jnp.zeros((8 - batch - 1, d), F32)], axis=0)
    mod4 = _mod_call(cc, w_mod, b_mod)

    w_gu = w_gate_up.astype(BF16)
    w_dn = w_down.astype(BF16)
    w_ab = w_in_ab.astype(BF16)
    w_oab = w_out_ab.astype(BF16)
    w_oc = w_out_c.astype(BF16)

    tables_h = _rope_tables(n, HEAD_DIM, HEAD_DIM)
    tables_m = _rope_tables(n, MLA_ROPE, 128)

    xs = x.reshape(batch * n, d)
    hs = ctx.reshape(batch * n_ctx, d)
    x_mod = dict(mod_base=0, rows_per_mod=n)
    c_mod = dict(mod_base=batch, rows_per_mod=batch * n_ctx)

    for l in range(depth):
        need_ctx = l < depth - 1
        last = l == depth - 1
        i = l // 2
        xs = _ffn_call(xs, mod4, g_norm, w_gu, w_dn, l=l, jf=0, jm=0, **x_mod)
        hs = _ffn_call(hs, mod4, g_norm, w_gu, w_dn, l=l, jf=0, jm=0, **c_mod)
        if l % 2 == 0:
            w = w_ab[i]
            w_qk = jnp.concatenate([w[:, :W_KA], w[:, W_QB:W_KB], w[:, W_KA:W_VA], w[:, W_KB:W_VB]], axis=1)
            w_vt = jnp.concatenate([w[:, W_VA:W_QB], w[:, W_VB:]], axis=1).T
            qkv_x, vt_x = _proj_ab_call(xs, mod4, g_norm, w_qk, w_vt, g_qnorm_a[i], g_knorm_a[i], tables_h,
                                        l=l, n_pos=n, **x_mod)
            qkv_c, vt_c = _proj_ab_call(hs, mod4, g_norm, w_qk, w_vt, g_qnorm_a[i], g_knorm_a[i], None,
                                        l=l, n_pos=n, **c_mod)
            oa = _flash_call(qkv_x, (qkv_c, vt_c), (qkv_x, vt_x), None, batch=batch, n_q=n, n_ctx=n_ctx,
                             n_heads=N_HEADS_A, group=GROUP_A, dq=HEAD_DIM, dv=HEAD_DIM,
                             q_col=COL_QA, k_col=COL_KA, v_col=0, tq=min(256, n))
            ob = _window_call(qkv_x, vt_x, qkv_c, vt_c, sink_b[i], batch=batch, n=n, n_ctx=n_ctx)
            w_o = w_oab[i]
            xs_new = _out_call(xs, mod4, [(oa, w_o[:QA]), (ob, w_o[QA:])], l=l, **x_mod)
            if need_ctx:
                oa_c = _flash_call(qkv_c, (qkv_c, vt_c), None, None, batch=batch, n_q=n_ctx, n_ctx=n_ctx,
                                   n_heads=N_HEADS_A, group=GROUP_A, dq=HEAD_DIM, dv=HEAD_DIM,
                                   q_col=COL_QA, k_col=COL_KA, v_col=0, tq=min(128, n_ctx))
                ob_c = _flash_call(qkv_c, (qkv_c, vt_c), None, sink_b[i], batch=batch, n_q=n_ctx, n_ctx=n_ctx,
                                   n_heads=N_HEADS_B, group=GROUP_B, dq=HEAD_DIM, dv=HEAD_DIM,
                                   q_col=COL_QB, k_col=COL_KB, v_col=N_KV_A, tq=min(128, n_ctx))
                hs = _out_call(hs, mod4, [(oa_c, w_o[:QA]), (ob_c, w_o[QA:])], l=l, **c_mod)
            xs = xs_new
        else:
            w_in, wq, wk, wvt = _mla_weight_layout(w_in_c[i], w_uq[i], w_ukv[i])
            q_x, k_x, v_x = _proj_mla_call(xs, mod4, g_norm, w_in, g_cq[i], g_ckv[i], wq, wk, wvt, tables_m,
                                           l=l, n_pos=n, **x_mod)
            q_c, k_c, v_c = _proj_mla_call(hs, mod4, g_norm, w_in, g_cq[i], g_ckv[i], wq, wk, wvt, None,
                                           l=l, n_pos=n, **c_mod)
            o = _flash_call(q_x, (k_c, v_c), (k_x, v_x), None, batch=batch, n_q=n, n_ctx=n_ctx,
                            n_heads=MLA_HEADS, group=1, dq=MLA_QK, dv=MLA_V, q_col=0, k_col=0, v_col=0,
                            tq=min(1024, n))
            xs_new = _out_call(xs, mod4, [(o, w_oc[i])], l=l, **x_mod)
            if need_ctx:
                o_c = _flash_call(q_c, (k_c, v_c), None, None, batch=batch, n_q=n_ctx, n_ctx=n_ctx,
                                  n_heads=MLA_HEADS, group=1, dq=MLA_QK, dv=MLA_V, q_col=0, k_col=0, v_col=0,
                                  tq=n_ctx)
                hs = _out_call(hs, mod4, [(o_c, w_oc[i])], l=l, **c_mod)
            xs = xs_new
        xs = _ffn_call(xs, mod4, g_norm, w_gu, w_dn, l=l, jf=1, jm=2, g_final=g_final if last else None, **x_mod)
        if need_ctx:
            hs = _ffn_call(hs, mod4, g_norm, w_gu, w_dn, l=l, jf=1, jm=2, **c_mod)
    return xs.reshape(batch, n, d)
```
